```python
import math
import jax
import jax.numpy as jnp
from jax import lax
import numpy as np

D_MODEL = 2048
BATCH = 8
SEQ = 4096
DEPTH = 2
DEC_BATCH = 8
DEC_SEQ = 64
PAST_LEN = 4096

CHUNK = 64
N_META = 16
Q_BLOCK = 128
FOX_HEADS = 8
FOX_HEAD_DIM = D_MODEL // 16
FOX_WIDTH = FOX_HEADS * FOX_HEAD_DIM
DIFF_HEADS = 4
DIFF_HEAD_DIM = D_MODEL // 16
DIFF_V_DIM = 2 * DIFF_HEAD_DIM
DIFF_QK_WIDTH = DIFF_HEADS * 2 * DIFF_HEAD_DIM
DIFF_WIDTH = DIFF_HEADS * DIFF_V_DIM
ROT_DIM = DIFF_HEAD_DIM // 4
ROPE_THETA = 500000.0
D_FF = 4 * D_MODEL
SPLIT_SIZES = (FOX_WIDTH, FOX_WIDTH, FOX_WIDTH, FOX_HEADS,
               DIFF_QK_WIDTH, DIFF_QK_WIDTH, DIFF_WIDTH, D_MODEL, D_MODEL)
N_IN = sum(SPLIT_SIZES)
DEEPNORM_ALPHA = (2 * DEPTH) ** 0.25
DEEPNORM_BETA = (8 * DEPTH) ** -0.25
FOX_SCALE = FOX_HEAD_DIM ** -0.5
DIFF_SCALE = DIFF_HEAD_DIM ** -0.5
LN_EPS = 1e-5
RMS_EPS = 1e-5
NEG = -1e30

kernel_name = 'fox_diff_gated_hybrid_stream_step'


def layer_norm(x, g, b):
    xf = x.astype(jnp.float32)
    mu = jnp.mean(xf, axis=-1, keepdims=True)
    var = jnp.mean(jnp.square(xf - mu), axis=-1, keepdims=True)
    y = (xf - mu) * lax.rsqrt(var + LN_EPS) * g.astype(jnp.float32) + b.astype(jnp.float32)
    return y.astype(x.dtype)


def partial_rope(x, pos):
    half = ROT_DIM // 2
    inv_freq = ROPE_THETA ** (-jnp.arange(half, dtype=jnp.float32) / half)
    ang = pos.astype(jnp.float32)[:, None] * inv_freq[None, :]
    bshape = (pos.shape[0],) + (1,) * (x.ndim - 3) + (half,)
    cos = jnp.cos(ang).reshape(bshape)
    sin = jnp.sin(ang).reshape(bshape)
    xr = x[..., :ROT_DIM].astype(jnp.float32)
    x1, x2 = xr[..., :half], xr[..., half:]
    rot = jnp.concatenate([x1 * cos - x2 * sin, x2 * cos + x1 * sin], axis=-1)
    return jnp.concatenate([rot.astype(x.dtype), x[..., ROT_DIM:]], axis=-1)


def chunk_id(idx):
    return jnp.where(idx < N_META, 0, 1 + (idx - N_META) // CHUNK)


def project(x, w_in, b_f, pos):
    B, L, _ = x.shape
    z = jnp.einsum('bld,dn->bln', x, w_in)
    points = []
    acc = 0
    for n in SPLIT_SIZES[:-1]:
        acc += n
        points.append(acc)
    qa, ka, va, fa, qb, kb, vb, ga, gb = jnp.split(z, points, axis=-1)
    qa = qa.reshape(B, L, FOX_HEADS, FOX_HEAD_DIM)
    ka = ka.reshape(B, L, FOX_HEADS, FOX_HEAD_DIM)
    va = va.reshape(B, L, FOX_HEADS, FOX_HEAD_DIM)
    logf = jax.nn.log_sigmoid(fa.astype(jnp.float32) + b_f.astype(jnp.float32))
    qb = partial_rope(qb.reshape(B, L, DIFF_HEADS, 2, DIFF_HEAD_DIM), pos)
    kb = partial_rope(kb.reshape(B, L, DIFF_HEADS, 2, DIFF_HEAD_DIM), pos)
    vb = vb.reshape(B, L, DIFF_HEADS, DIFF_V_DIM)
    return qa, ka, va, logf, qb, kb, vb, ga, gb


def fox_block(q, cq, pos_q, k, v, ck, pos_k):
    s = jnp.einsum('bqhd,bkhd->bhqk', q, k, preferred_element_type=jnp.float32) * FOX_SCALE
    s = s + (jnp.swapaxes(cq, 1, 2)[..., :, None] - jnp.swapaxes(ck, 1, 2)[..., None, :])
    s = jnp.where(pos_k[None, :] <= pos_q[:, None], s, NEG)
    p = jax.nn.softmax(s, axis=-1)
    return jnp.einsum('bhqk,bkhd->bqhd', p.astype(v.dtype), v)


def diff_block(q, k, v, lam, allowed):
    s = jnp.einsum('bqhcd,bkhcd->bchqk', q, k, preferred_element_type=jnp.float32) * DIFF_SCALE
    if allowed is not None:
        s = jnp.where(allowed, s, NEG)
    p = jax.nn.softmax(s, axis=-1)
    a = p[:, 0] - lam * p[:, 1]
    return jnp.einsum('bhqk,bkhe->bqhe', a.astype(v.dtype), v)


def diff_lambda(layer, lq1, lk1, lq2, lk2):
    lam_init = 0.8 - 0.6 * math.exp(-0.3 * layer)
    d1 = jnp.sum(lq1.astype(jnp.float32) * lk1.astype(jnp.float32))
    d2 = jnp.sum(lq2.astype(jnp.float32) * lk2.astype(jnp.float32))
    return jnp.exp(d1) - jnp.exp(d2) + lam_init, lam_init


def diff_output(o, g, lam_init):
    B, L = o.shape[:2]
    of = o.astype(jnp.float32)
    of = of * lax.rsqrt(jnp.mean(of * of, axis=-1, keepdims=True) + RMS_EPS) * g.astype(jnp.float32)
    return (of * (1.0 - lam_init)).astype(o.dtype).reshape(B, L, DIFF_WIDTH)


def to_blocks(a):
    L = a.shape[1]
    nb = -(-L // Q_BLOCK)
    pad = [(0, 0)] * a.ndim
    pad[1] = (0, nb * Q_BLOCK - L)
    a = jnp.pad(a, pad)
    return jnp.moveaxis(a.reshape((a.shape[0], nb, Q_BLOCK) + a.shape[2:]), 1, 0)


def from_blocks(o, L):
    o = jnp.moveaxis(o, 0, 1)
    return o.reshape((o.shape[0], -1) + o.shape[3:])[:, :L]


def merge_ffn(x, oa, ob, ga, gb, w_br_a, w_br_b, w_out, ln1_g, ln1_b, w_up, w_down, ln2_g, ln2_b):
    m = jax.nn.sigmoid(ga) * (oa @ w_br_a) + jax.nn.sigmoid(gb) * (ob @ w_br_b)
    x = layer_norm(DEEPNORM_ALPHA * x + m @ w_out, ln1_g, ln1_b)
    hid = jnp.square(jax.nn.relu(x @ w_up)) @ w_down
    return layer_norm(DEEPNORM_ALPHA * x + hid, ln2_g, ln2_b)


def setup_inputs(seed: int = 0) -> dict:
    key = jax.random.key(seed)
    ks = jax.random.split(key, 32)
    f32 = jnp.float32

    def nrm(k, shape, scale):
        return scale * jax.random.normal(k, shape, f32)

    forget_profile = jnp.linspace(1.0, 6.0, FOX_HEADS, dtype=f32)
    return {
        'x_prompt': nrm(ks[0], (BATCH, SEQ, D_MODEL), 1.0),
        'x_sample': nrm(ks[1], (DEC_BATCH, DEC_SEQ, D_MODEL), 1.0),
        'cache_fox_k': nrm(ks[2], (DEPTH, DEC_BATCH, PAST_LEN, FOX_HEADS, FOX_HEAD_DIM), 1.0),
        'cache_fox_v': nrm(ks[3], (DEPTH, DEC_BATCH, PAST_LEN, FOX_HEADS, FOX_HEAD_DIM), 1.0),
        'cache_fox_logf': jax.nn.log_sigmoid(nrm(ks[4], (DEPTH, DEC_BATCH, PAST_LEN, FOX_HEADS), 1.0) + forget_profile),
        'cache_diff_k': nrm(ks[5], (DEPTH, DEC_BATCH, PAST_LEN, DIFF_HEADS, 2 * DIFF_HEAD_DIM), 1.0),
        'cache_diff_v': nrm(ks[6], (DEPTH, DEC_BATCH, PAST_LEN, DIFF_HEADS, DIFF_V_DIM), 1.0),
        'meta_tokens': nrm(ks[7], (N_META, D_MODEL), 1.0),
        'ln_in_g': 1.0 + nrm(ks[8], (D_MODEL,), 0.02),
        'ln_in_b': nrm(ks[9], (D_MODEL,), 0.02),
        'w_in': nrm(ks[10], (DEPTH, D_MODEL, N_IN), D_MODEL ** -0.5),
        'b_f': forget_profile + nrm(ks[11], (DEPTH, FOX_HEADS), 0.1),
        'lambda_q1': nrm(ks[12], (DEPTH, DIFF_HEAD_DIM), 0.1),
        'lambda_k1': nrm(ks[13], (DEPTH, DIFF_HEAD_DIM), 0.1),
        'lambda_q2': nrm(ks[14], (DEPTH, DIFF_HEAD_DIM), 0.1),
        'lambda_k2': nrm(ks[15], (DEPTH, DIFF_HEAD_DIM), 0.1),
        'subln_g': 1.0 + nrm(ks[16], (DEPTH, DIFF_V_DIM), 0.02),
        'w_br_a': nrm(ks[17], (DEPTH, FOX_WIDTH, D_MODEL), DEEPNORM_BETA * FOX_WIDTH ** -0.5),
        'w_br_b': nrm(ks[18], (DEPTH, DIFF_WIDTH, D_MODEL), DEEPNORM_BETA * DIFF_WIDTH ** -0.5),
        'w_out': nrm(ks[19], (DEPTH, D_MODEL, D_MODEL), DEEPNORM_BETA * D_MODEL ** -0.5),
        'ln1_g': 1.0 + nrm(ks[20], (DEPTH, D_MODEL), 0.02),
        'ln1_b': nrm(ks[21], (DEPTH, D_MODEL), 0.02),
        'w_up': nrm(ks[22], (DEPTH, D_MODEL, D_FF), D_MODEL ** -0.5),
        'w_down': nrm(ks[23], (DEPTH, D_FF, D_MODEL), DEEPNORM_BETA * D_FF ** -0.5),
        'ln2_g': 1.0 + nrm(ks[24], (DEPTH, D_MODEL), 0.02),
        'ln2_b': nrm(ks[25], (DEPTH, D_MODEL), 0.02),
    }


def reference(x_prompt, x_sample, cache_fox_k, cache_fox_v, cache_fox_logf, cache_diff_k, cache_diff_v,
              meta_tokens, ln_in_g, ln_in_b, w_in, b_f, lambda_q1, lambda_k1, lambda_q2, lambda_k2,
              subln_g, w_br_a, w_br_b, w_out, ln1_g, ln1_b, w_up, w_down, ln2_g, ln2_b):
    B = x_prompt.shape[0]
    Bd, T = x_sample.shape[0], x_sample.shape[1]
    P = cache_fox_k.shape[2]

    meta = jnp.broadcast_to(meta_tokens[None].astype(x_prompt.dtype), (B, N_META, D_MODEL))
    h = layer_norm(jnp.concatenate([meta, x_prompt], axis=1), ln_in_g, ln_in_b)
    L = h.shape[1]
    n_blocks = -(-L // Q_BLOCK)
    pos_p = jnp.arange(L)
    qidx = jnp.arange(n_blocks * Q_BLOCK).reshape(n_blocks, Q_BLOCK)
    chunk_p = chunk_id(pos_p)
    chunk_qb = chunk_id(qidx)

    s = layer_norm(x_sample, ln_in_g, ln_in_b)
    pos_s = P + jnp.arange(T)
    pos_sk = jnp.arange(P + T)

    fk_p, fv_p, fl_p, dk_p, dv_p = [], [], [], [], []
    fk_s, fv_s, fl_s, dk_s, dv_s = [], [], [], [], []
    for l in range(DEPTH):
        lam, lam_init = diff_lambda(l, lambda_q1[l], lambda_k1[l], lambda_q2[l], lambda_k2[l])

        qa, ka, va, logf, qb, kb, vb, ga, gb = project(h, w_in[l], b_f[l], pos_p)
        c = jnp.cumsum(logf, axis=1)
        oa = lax.map(lambda a: fox_block(a[0], a[1], a[2], ka, va, c, pos_p),
                     (to_blocks(qa), to_blocks(c), qidx))
        oa = from_blocks(oa, L).reshape(B, L, FOX_WIDTH)
        ob = lax.map(lambda a: diff_block(a[0], kb, vb, lam, chunk_p[None, :] <= a[1][:, None]),
                     (to_blocks(qb), chunk_qb))
        ob = diff_output(from_blocks(ob, L), subln_g[l], lam_init)
        fk_p.append(ka)
        fv_p.append(va)
        fl_p.append(logf)
        dk_p.append(kb.reshape(B, L, DIFF_HEADS, 2 * DIFF_HEAD_DIM))
        dv_p.append(vb)
        h = merge_ffn(h, oa, ob, ga, gb, w_br_a[l], w_br_b[l], w_out[l], ln1_g[l], ln1_b[l],
                      w_up[l], w_down[l], ln2_g[l], ln2_b[l])

        qa2, ka2, va2, logf2, qb2, kb2, vb2, ga2, gb2 = project(s, w_in[l], b_f[l], pos_s)
        k_all = jnp.concatenate([cache_fox_k[l], ka2], axis=1)
        v_all = jnp.concatenate([cache_fox_v[l], va2], axis=1)
        c_all = jnp.cumsum(jnp.concatenate([cache_fox_logf[l].astype(jnp.float32), logf2], axis=1), axis=1)
        oa2 = fox_block(qa2, c_all[:, P:], pos_s, k_all, v_all, c_all, pos_sk).reshape(Bd, T, FOX_WIDTH)
        kb_all = jnp.concatenate([cache_diff_k[l].reshape(Bd, P, DIFF_HEADS, 2, DIFF_HEAD_DIM), kb2], axis=1)
        vb_all = jnp.concatenate([cache_diff_v[l], vb2], axis=1)
        ob2 = diff_output(diff_block(qb2, kb_all, vb_all, lam, None), subln_g[l], lam_init)
        fk_s.append(ka2)
        fv_s.append(va2)
        fl_s.append(logf2)
        dk_s.append(kb2.reshape(Bd, T, DIFF_HEADS, 2 * DIFF_HEAD_DIM))
        dv_s.append(vb2)
        s = merge_ffn(s, oa2, ob2, ga2, gb2, w_br_a[l], w_br_b[l], w_out[l], ln1_g[l], ln1_b[l],
                      w_up[l], w_down[l], ln2_g[l], ln2_b[l])

    y_prompt = h[:, N_META:]
    y_sample = s
    return (y_prompt, y_sample,
            jnp.stack(fk_p), jnp.stack(fv_p), jnp.stack(fl_p), jnp.stack(dk_p), jnp.stack(dv_p),
            jnp.stack(fk_s), jnp.stack(fv_s), jnp.stack(fl_s), jnp.stack(dk_s), jnp.stack(dv_s))
```

```python
import functools
import math

import jax
import jax.numpy as jnp
from jax import lax
from jax.experimental import pallas as pl
from jax.experimental.pallas import tpu as pltpu

N_META = 16
CHUNK = 64
FOX_HEADS = 8
DIFF_HEADS = 4
HEAD_DIM = 128
ROT_DIM = 32
ROPE_THETA = 500000.0
LN_EPS = 1e-5
RMS_EPS = 1e-5
NEG = -1e30

LANES = 128
VMEM_LIMIT_BYTES = 56 * 2**20

ROW_TILE = 512
ATT_TILE = 512
FF_TILE = 1024
COL_CHUNK = 512

F32 = jnp.float32
BF16 = jnp.bfloat16


def _params(*sem):
    return pltpu.CompilerParams(dimension_semantics=sem, vmem_limit_bytes=VMEM_LIMIT_BYTES)


def _resident(shape):
    return pl.BlockSpec(shape, lambda *_: (0,) * len(shape), pipeline_mode=pl.Buffered(1))


def _layer_norm(x, g, b):
    mu = jnp.mean(x, axis=-1, keepdims=True)
    xc = x - mu
    var = jnp.mean(xc * xc, axis=-1, keepdims=True)
    return xc * lax.rsqrt(var + LN_EPS) * g + b


def _dot(a, b):
    return jnp.dot(a, b, preferred_element_type=F32)


def _dot_nt(a, b):
    return lax.dot_general(a, b, (((1,), (1,)), ((), ())), preferred_element_type=F32)


def _ln_kernel(x_ref, g_ref, b_ref, o_ref):
    o_ref[...] = _layer_norm(x_ref[...], g_ref[...], b_ref[...])


def _input_layer_norm(x, g, b):
    t, d = x.shape
    tm = min(ROW_TILE, t)
    row = pl.BlockSpec((tm, d), lambda i: (i, 0))
    vec = pl.BlockSpec((1, d), lambda i: (0, 0))
    return pl.pallas_call(
        _ln_kernel, grid=(pl.cdiv(t, tm),), in_specs=[row, vec, vec], out_specs=row,
        out_shape=jax.ShapeDtypeStruct((t, d), F32), compiler_params=_params("parallel"),
        name="input_ln")(x, g.reshape(1, d), b.reshape(1, d))


def _proj_fox_kernel(h_ref, w_ref, wf_ref, bf_ref, q_ref, k_ref, v_ref, lf_ref):
    hb = h_ref[...].astype(BF16)
    w = q_ref.shape[1]
    q_ref[...] = _dot(hb, w_ref[:, 0:w]).astype(BF16)
    k_ref[...] = _dot(hb, w_ref[:, w:2 * w])
    v_ref[...] = _dot(hb, w_ref[:, 2 * w:3 * w])
    fa = _dot(hb, wf_ref[...]) + bf_ref[...]
    lf = jnp.minimum(fa, 0.0) - jnp.log1p(jnp.exp(-jnp.abs(fa)))
    lf_ref[...] = lf[:, :lf_ref.shape[1]]


def _project_fox(h, w, wf, bf):
    t, d = h.shape
    width = w.shape[1] // 3
    tm = min(ROW_TILE, t)
    rows = lambda n: pl.BlockSpec((tm, n), lambda i: (i, 0))
    return pl.pallas_call(
        _proj_fox_kernel, grid=(pl.cdiv(t, tm),),
        in_specs=[rows(d), _resident(w.shape), _resident(wf.shape), _resident(bf.shape)],
        out_specs=[rows(width), rows(width), rows(width), rows(FOX_HEADS)],
        out_shape=[jax.ShapeDtypeStruct((t, width), BF16), jax.ShapeDtypeStruct((t, width), F32),
                   jax.ShapeDtypeStruct((t, width), F32), jax.ShapeDtypeStruct((t, FOX_HEADS), F32)],
        compiler_params=_params("parallel"), name="proj_fox")(h, w, wf, bf)


def _proj_diff_kernel(h_ref, w_ref, cs_ref, q_ref, k_ref, v_ref):
    hb = h_ref[...].astype(BF16)
    tm, w = q_ref.shape
    cos = cs_ref[:, 0:LANES]
    sin = cs_ref[:, LANES:2 * LANES]
    half = ROT_DIM // 2
    first = lax.broadcasted_iota(jnp.int32, (tm, LANES), 1) < half

    def rope(x):
        partner = jnp.where(first, pltpu.roll(x, LANES - half, 1), pltpu.roll(x, half, 1))
        return x * cos + partner * sin

    for part, ref in enumerate((q_ref, k_ref)):
        z = _dot(hb, w_ref[:, part * w:(part + 1) * w])
        for g in range(w // LANES):
            sl = slice(g * LANES, (g + 1) * LANES)
            ref[:, sl] = rope(z[:, sl]).astype(ref.dtype)
    v_ref[...] = _dot(hb, w_ref[:, 2 * w:3 * w])


def _project_diff(h, w, cs):
    t, d = h.shape
    width = w.shape[1] // 3
    tm = min(ROW_TILE, t)
    rows = lambda n: pl.BlockSpec((tm, n), lambda i: (i, 0))
    return pl.pallas_call(
        _proj_diff_kernel, grid=(pl.cdiv(t, tm),),
        in_specs=[rows(d), _resident(w.shape), rows(2 * LANES)],
        out_specs=[rows(width), rows(width), rows(width)],
        out_shape=[jax.ShapeDtypeStruct((t, width), BF16), jax.ShapeDtypeStruct((t, width), F32),
                   jax.ShapeDtypeStruct((t, width), F32)],
        compiler_params=_params("parallel"), name="proj_diff")(h, w, cs)


def _rope_table(pos):
    half = ROT_DIM // 2
    inv_freq = ROPE_THETA ** (-jnp.arange(half, dtype=F32) / half)
    ang = pos.astype(F32)[:, None] * inv_freq[None, :]
    cos, sin = jnp.cos(ang), jnp.sin(ang)
    n = pos.shape[0]
    ones = jnp.ones((n, LANES - ROT_DIM), F32)
    zeros = jnp.zeros((n, LANES - ROT_DIM), F32)
    return jnp.concatenate([cos, cos, ones, -sin, sin, zeros], axis=1)


def _cumsum_kernel(x_ref, o_ref):
    rows, n = x_ref.shape
    lane = lax.broadcasted_iota(jnp.int32, (rows, LANES), 1)
    carry = jnp.zeros((rows, 1), F32)
    for j in range(n // LANES):
        sl = slice(j * LANES, (j + 1) * LANES)
        v = x_ref[:, sl]
        shift = 1
        while shift < LANES:
            v = v + jnp.where(lane >= shift, pltpu.roll(v, shift, 1), 0.0)
            shift *= 2
        v = v + carry
        o_ref[:, sl] = v
        carry = v[:, LANES - 1:LANES]


def _cumsum_lanes(x):
    b, r, n = x.shape
    spec = pl.BlockSpec((None, r, n), lambda i: (i, 0, 0))
    return pl.pallas_call(
        _cumsum_kernel, grid=(b,), in_specs=[spec], out_specs=spec,
        out_shape=jax.ShapeDtypeStruct(x.shape, F32), compiler_params=_params("parallel"),
        name="logf_cumsum")(x)


def _cumulative_logf(logf):
    b, n, r = logf.shape
    n_pad = pl.cdiv(n, LANES) * LANES
    x = jnp.pad(jnp.transpose(logf, (0, 2, 1)), ((0, 0), (0, 0), (0, n_pad - n)))
    return _cumsum_lanes(x)[:, :, :n]


class _Softmax:
    def __init__(self, m_ref, l_ref, acc_ref, rows):
        self.m, self.l, self.acc, self.rows = m_ref, l_ref, acc_ref, rows

    def reset(self):
        r = self.rows
        self.m[0:r] = jnp.full((r, 1), NEG, F32)
        self.l[0:r] = jnp.zeros((r, 1), F32)
        self.acc[0:r] = jnp.zeros((r, self.acc.shape[1]), F32)

    def update(self, z, v):
        r = self.rows
        m_prev = self.m[0:r]
        m_new = jnp.maximum(m_prev, jnp.max(z, axis=1, keepdims=True))
        scale = jnp.exp(m_prev - m_new)
        p = jnp.exp(z - m_new)
        self.l[0:r] = scale * self.l[0:r] + jnp.sum(p, axis=1, keepdims=True)
        self.acc[0:r] = scale * self.acc[0:r] + _dot(p.astype(BF16), v)
        self.m[0:r] = m_new

    def result(self):
        r = self.rows
        return self.acc[0:r] / self.l[0:r]


def _softmax_scratch(rows, dv):
    return [pltpu.VMEM((rows, 1), F32), pltpu.VMEM((rows, 1), F32), pltpu.VMEM((rows, dv), F32)]


def _fox_prompt_kernel(q_ref, k_ref, v_ref, cm_ref, cs_ref, o_ref, kb, vb, m_ref, l_ref, acc_ref, *, tq):
    n = q_ref.shape[0]
    s_len = n - N_META
    scale = HEAD_DIM ** -0.5
    kb[0:s_len] = k_ref[N_META:n].astype(BF16)
    vb[0:s_len] = v_ref[N_META:n].astype(BF16)
    kb[s_len:n] = k_ref[0:N_META].astype(BF16)
    vb[s_len:n] = v_ref[0:N_META].astype(BF16)
    k_meta, v_meta, c_meta = kb[s_len:n], vb[s_len:n], cm_ref[...]

    def causal(rows):
        return (lax.broadcasted_iota(jnp.int32, (rows, rows), 0)
                >= lax.broadcasted_iota(jnp.int32, (rows, rows), 1))

    sm = _Softmax(m_ref, l_ref, acc_ref, N_META)
    sm.reset()
    z = _dot_nt(q_ref[0:N_META], k_meta) * scale - c_meta
    sm.update(jnp.where(causal(N_META), z, NEG), v_meta)
    o_ref[0:N_META] = sm.result().astype(o_ref.dtype)

    sq = _Softmax(m_ref, l_ref, acc_ref, tq)
    diag = causal(tq)

    def q_block(i, carry):
        r0 = pl.multiple_of(i * tq, tq)
        q = q_ref[pl.ds(N_META + r0, tq)]
        sq.reset()
        sq.update(_dot_nt(q, k_meta) * scale - c_meta, v_meta)

        def kv_block(j, c):
            c0 = pl.multiple_of(j * tq, tq)
            z = _dot_nt(q, kb[pl.ds(c0, tq)]) * scale - cs_ref[pl.ds(j, 1)]
            sq.update(z, vb[pl.ds(c0, tq)])
            return c

        lax.fori_loop(0, i, kv_block, 0)
        z = _dot_nt(q, kb[pl.ds(r0, tq)]) * scale - cs_ref[pl.ds(i, 1)]
        sq.update(jnp.where(diag, z, NEG), vb[pl.ds(r0, tq)])
        o_ref[pl.ds(N_META + r0, tq)] = sq.result().astype(o_ref.dtype)
        return carry

    lax.fori_loop(0, s_len // tq, q_block, 0)


def _fox_prompt(q, k, v, c):
    b, n, width = q.shape
    heads = width // HEAD_DIM
    s_len = n - N_META
    tq = min(ATT_TILE, s_len)
    c_meta = c[:, :, None, :N_META]
    c_main = c[:, :, N_META:].reshape(b, heads, s_len // tq, tq)
    head = pl.BlockSpec((None, n, HEAD_DIM), lambda i, h: (i, 0, h))
    return pl.pallas_call(
        functools.partial(_fox_prompt_kernel, tq=tq), grid=(b, heads),
        in_specs=[head, head, head,
                  pl.BlockSpec((None, None, 1, N_META), lambda i, h: (i, h, 0, 0)),
                  pl.BlockSpec((None, None, s_len // tq, tq), lambda i, h: (i, h, 0, 0))],
        out_specs=head, out_shape=jax.ShapeDtypeStruct(q.shape, BF16),
        scratch_shapes=[pltpu.VMEM((n, HEAD_DIM), BF16), pltpu.VMEM((n, HEAD_DIM), BF16)]
        + _softmax_scratch(tq, HEAD_DIM),
        compiler_params=_params("parallel", "parallel"), name="fox_prompt")(q, k, v, c_meta, c_main)


def _fox_sample_kernel(q_ref, kc_ref, vc_ref, kn_ref, vn_ref, cc_ref, cn_ref, o_ref, m_ref, l_ref, acc_ref, *, tk):
    t = q_ref.shape[0]
    scale = HEAD_DIM ** -0.5
    q = q_ref[...]
    sm = _Softmax(m_ref, l_ref, acc_ref, t)
    sm.reset()

    def cache_block(j, c):
        c0 = pl.multiple_of(j * tk, tk)
        z = _dot_nt(q, kc_ref[pl.ds(c0, tk)].astype(BF16)) * scale - cc_ref[pl.ds(j, 1)]
        sm.update(z, vc_ref[pl.ds(c0, tk)].astype(BF16))
        return c

    lax.fori_loop(0, kc_ref.shape[0] // tk, cache_block, 0)
    z = _dot_nt(q, kn_ref[...].astype(BF16)) * scale - cn_ref[...]
    mask = lax.broadcasted_iota(jnp.int32, (t, t), 0) >= lax.broadcasted_iota(jnp.int32, (t, t), 1)
    sm.update(jnp.where(mask, z, NEG), vn_ref[...].astype(BF16))
    o_ref[...] = sm.result().astype(o_ref.dtype)


def _fox_sample(q, cache_k, cache_v, layer, k_new, v_new, c):
    b, t, width = q.shape
    heads = width // HEAD_DIM
    p = cache_k.shape[2]
    tk = min(ATT_TILE, p)
    c_cache = c[:, :, :p].reshape(b, heads, p // tk, tk)
    c_new = c[:, :, None, p:]
    new = pl.BlockSpec((None, t, HEAD_DIM), lambda i, h: (i, 0, h))
    cache = pl.BlockSpec((None, None, p, HEAD_DIM), lambda i, h: (layer, i, 0, h))
    return pl.pallas_call(
        functools.partial(_fox_sample_kernel, tk=tk), grid=(b, heads),
        in_specs=[new, cache, cache, new, new,
                  pl.BlockSpec((None, None, p // tk, tk), lambda i, h: (i, h, 0, 0)),
                  pl.BlockSpec((None, None, 1, t), lambda i, h: (i, h, 0, 0))],
        out_specs=new, out_shape=jax.ShapeDtypeStruct(q.shape, BF16),
        scratch_shapes=_softmax_scratch(t, HEAD_DIM),
        compiler_params=_params("parallel", "parallel"), name="fox_sample")(
            q, cache_k, cache_v, k_new, v_new, c_cache, c_new)


def _diff_lambda(lq1_ref, lk1_ref, lq2_ref, lk2_ref, lam_init):
    d1 = jnp.sum(lq1_ref[...] * lk1_ref[...], axis=1, keepdims=True)
    d2 = jnp.sum(lq2_ref[...] * lk2_ref[...], axis=1, keepdims=True)
    return jnp.exp(d1) - jnp.exp(d2) + lam_init


def _diff_finish(s1, s2, lam, g_ref, lam_init):
    o = s1.result() - lam * s2.result()
    o = o * lax.rsqrt(jnp.mean(o * o, axis=1, keepdims=True) + RMS_EPS) * g_ref[...]
    return o * (1.0 - lam_init)


def _diff_prompt_kernel(q_ref, k_ref, v_ref, lq1_ref, lk1_ref, lq2_ref, lk2_ref, g_ref, o_ref,
                        kb, vb, m1, l1, a1, m2, l2, a2, *, tq, lam_init):
    n = q_ref.shape[0]
    s_len = n - N_META
    d = HEAD_DIM
    scale = d ** -0.5
    lam = _diff_lambda(lq1_ref, lk1_ref, lq2_ref, lk2_ref, lam_init)
    kb[0:s_len] = k_ref[N_META:n].astype(BF16)
    vb[0:s_len] = v_ref[N_META:n].astype(BF16)
    kb[s_len:n] = k_ref[0:N_META].astype(BF16)
    vb[s_len:n] = v_ref[0:N_META].astype(BF16)
    k_meta, v_meta = kb[s_len:n], vb[s_len:n]

    def both(s1, s2, q, k, v, mask):
        for sm, lo in ((s1, 0), (s2, d)):
            z = _dot_nt(q[:, lo:lo + d], k[:, lo:lo + d]) * scale
            sm.update(z if mask is None else jnp.where(mask, z, NEG), v)

    s1, s2 = _Softmax(m1, l1, a1, N_META), _Softmax(m2, l2, a2, N_META)
    s1.reset()
    s2.reset()
    both(s1, s2, q_ref[0:N_META], k_meta, v_meta, None)
    o_ref[0:N_META] = _diff_finish(s1, s2, lam, g_ref, lam_init).astype(o_ref.dtype)

    t1, t2 = _Softmax(m1, l1, a1, tq), _Softmax(m2, l2, a2, tq)
    diag = (lax.broadcasted_iota(jnp.int32, (tq, tq), 0) // CHUNK
            >= lax.broadcasted_iota(jnp.int32, (tq, tq), 1) // CHUNK)

    def q_block(i, carry):
        r0 = pl.multiple_of(i * tq, tq)
        q = q_ref[pl.ds(N_META + r0, tq)]
        t1.reset()
        t2.reset()
        both(t1, t2, q, k_meta, v_meta, None)

        def kv_block(j, c):
            c0 = pl.multiple_of(j * tq, tq)
            both(t1, t2, q, kb[pl.ds(c0, tq)], vb[pl.ds(c0, tq)], None)
            return c

        lax.fori_loop(0, i, kv_block, 0)
        both(t1, t2, q, kb[pl.ds(r0, tq)], vb[pl.ds(r0, tq)], diag)
        o_ref[pl.ds(N_META + r0, tq)] = _diff_finish(t1, t2, lam, g_ref, lam_init).astype(o_ref.dtype)
        return carry

    lax.fori_loop(0, s_len // tq, q_block, 0)


def _lambda_specs():
    return [pl.BlockSpec((1, HEAD_DIM), lambda i, h: (0, 0))] * 4 + [
        pl.BlockSpec((1, 2 * HEAD_DIM), lambda i, h: (0, 0))]


def _diff_prompt(q, k, v, lam_params, g, lam_init):
    b, n, width = q.shape
    dv = 2 * HEAD_DIM
    heads = width // dv
    s_len = n - N_META
    tq = min(ATT_TILE, s_len)
    head = pl.BlockSpec((None, n, dv), lambda i, h: (i, 0, h))
    return pl.pallas_call(
        functools.partial(_diff_prompt_kernel, tq=tq, lam_init=lam_init), grid=(b, heads),
        in_specs=[head, head, head] + _lambda_specs(),
        out_specs=head, out_shape=jax.ShapeDtypeStruct(q.shape, BF16),
        scratch_shapes=[pltpu.VMEM((n, dv), BF16), pltpu.VMEM((n, dv), BF16)]
        + _softmax_scratch(tq, dv) + _softmax_scratch(tq, dv),
        compiler_params=_params("parallel", "parallel"), name="diff_prompt")(q, k, v, *lam_params, g)


def _diff_sample_kernel(q_ref, kc_ref, vc_ref, kn_ref, vn_ref, lq1_ref, lk1_ref, lq2_ref, lk2_ref, g_ref, o_ref,
                        m1, l1, a1, m2, l2, a2, *, tk, lam_init):
    t = q_ref.shape[0]
    d = HEAD_DIM
    scale = d ** -0.5
    lam = _diff_lambda(lq1_ref, lk1_ref, lq2_ref, lk2_ref, lam_init)
    q = q_ref[...]
    s1, s2 = _Softmax(m1, l1, a1, t), _Softmax(m2, l2, a2, t)
    s1.reset()
    s2.reset()

    def both(k, v):
        for sm, lo in ((s1, 0), (s2, d)):
            sm.update(_dot_nt(q[:, lo:lo + d], k[:, lo:lo + d]) * scale, v)

    def cache_block(j, c):
        c0 = pl.multiple_of(j * tk, tk)
        both(kc_ref[pl.ds(c0, tk)].astype(BF16), vc_ref[pl.ds(c0, tk)].astype(BF16))
        return c

    lax.fori_loop(0, kc_ref.shape[0] // tk, cache_block, 0)
    both(kn_ref[...].astype(BF16), vn_ref[...].astype(BF16))
    o_ref[...] = _diff_finish(s1, s2, lam, g_ref, lam_init).astype(o_ref.dtype)


def _diff_sample(q, cache_k, cache_v, layer, k_new, v_new, lam_params, g, lam_init):
    b, t, width = q.shape
    dv = 2 * HEAD_DIM
    heads = width // dv
    p = cache_k.shape[2]
    tk = min(ATT_TILE, p)
    new = pl.BlockSpec((None, t, dv), lambda i, h: (i, 0, h))
    cache = pl.BlockSpec((None, None, p, dv), lambda i, h: (layer, i, 0, h))
    return pl.pallas_call(
        functools.partial(_diff_sample_kernel, tk=tk, lam_init=lam_init), grid=(b, heads),
        in_specs=[new, cache, cache, new, new] + _lambda_specs(),
        out_specs=new, out_shape=jax.ShapeDtypeStruct(q.shape, BF16),
        scratch_shapes=_softmax_scratch(t, dv) + _softmax_scratch(t, dv),
        compiler_params=_params("parallel", "parallel"), name="diff_sample")(
            q, cache_k, cache_v, k_new, v_new, *lam_params, g)


def _merge_kernel(h_ref, oa_ref, ob_ref, wga_ref, wgb_ref, wa_ref, wb_ref, m_ref):
    hb = h_ref[...].astype(BF16)
    oa, ob = oa_ref[...], ob_ref[...]
    for c in range(m_ref.shape[1] // COL_CHUNK):
        sl = slice(c * COL_CHUNK, (c + 1) * COL_CHUNK)
        ga = jax.nn.sigmoid(_dot(hb, wga_ref[:, sl]))
        gb = jax.nn.sigmoid(_dot(hb, wgb_ref[:, sl]))
        m_ref[:, sl] = (ga * _dot(oa, wa_ref[:, sl]) + gb * _dot(ob, wb_ref[:, sl])).astype(m_ref.dtype)


def _merge(h, oa, ob, wga, wgb, wa, wb):
    t, d = h.shape
    tm = min(ROW_TILE, t)
    rows = lambda n: pl.BlockSpec((tm, n), lambda i: (i, 0))
    return pl.pallas_call(
        _merge_kernel, grid=(pl.cdiv(t, tm),),
        in_specs=[rows(d), rows(oa.shape[1]), rows(ob.shape[1]),
                  _resident(wga.shape), _resident(wgb.shape), _resident(wa.shape), _resident(wb.shape)],
        out_specs=rows(d), out_shape=jax.ShapeDtypeStruct((t, d), BF16),
        compiler_params=_params("parallel"), name="gated_merge")(h, oa, ob, wga, wgb, wa, wb)


def _out_ln_kernel(h_ref, m_ref, w_ref, g_ref, b_ref, o_ref, *, alpha):
    y = _dot(m_ref[...], w_ref[...])
    o_ref[...] = _layer_norm(alpha * h_ref[...] + y, g_ref[...], b_ref[...])


def _out_proj_ln(h, m, w, g, b, alpha):
    t, d = h.shape
    tm = min(ROW_TILE, t)
    row = pl.BlockSpec((tm, d), lambda i: (i, 0))
    vec = pl.BlockSpec((1, d), lambda i: (0, 0))
    return pl.pallas_call(
        functools.partial(_out_ln_kernel, alpha=alpha), grid=(pl.cdiv(t, tm),),
        in_specs=[row, row, _resident(w.shape), vec, vec], out_specs=row,
        out_shape=jax.ShapeDtypeStruct((t, d), F32),
        compiler_params=_params("parallel"), name="out_proj_ln")(h, m, w, g.reshape(1, d), b.reshape(1, d))


def _mlp_kernel(x_ref, wu_ref, wd_ref, g_ref, b_ref, o_ref, xb_ref, acc_ref, *, alpha):
    f = pl.program_id(1)

    @pl.when(f == 0)
    def _():
        xb_ref[...] = x_ref[...].astype(BF16)
        acc_ref[...] = jnp.zeros(acc_ref.shape, F32)

    u = jnp.maximum(_dot(xb_ref[...], wu_ref[...]), 0.0)
    acc_ref[...] += _dot((u * u).astype(BF16), wd_ref[...])

    @pl.when(f == pl.num_programs(1) - 1)
    def _():
        o_ref[...] = _layer_norm(alpha * x_ref[...] + acc_ref[...], g_ref[...], b_ref[...])


def _mlp_ln(x, wu, wd, g, b, alpha):
    t, d = x.shape
    dff = wu.shape[1]
    tm = min(ROW_TILE, t)
    tf = min(FF_TILE, dff)
    row = pl.BlockSpec((tm, d), lambda i, f: (i, 0))
    vec = pl.BlockSpec((1, d), lambda i, f: (0, 0))
    return pl.pallas_call(
        functools.partial(_mlp_kernel, alpha=alpha), grid=(pl.cdiv(t, tm), dff // tf),
        in_specs=[row, pl.BlockSpec((d, tf), lambda i, f: (0, f)), pl.BlockSpec((tf, d), lambda i, f: (f, 0)),
                  vec, vec],
        out_specs=row, out_shape=jax.ShapeDtypeStruct((t, d), F32),
        scratch_shapes=[pltpu.VMEM((tm, d), BF16), pltpu.VMEM((tm, d), F32)],
        compiler_params=_params("parallel", "arbitrary"), name="mlp_ln")(
            x, wu, wd, g.reshape(1, d), b.reshape(1, d))


def kernel(x_prompt, x_sample, cache_fox_k, cache_fox_v, cache_fox_logf, cache_diff_k, cache_diff_v, meta_tokens, ln_in_g, ln_in_b, w_in, b_f, lambda_q1, lambda_k1, lambda_q2, lambda_k2, subln_g, w_br_a, w_br_b, w_out, ln1_g, ln1_b, w_up, w_down, ln2_g, ln2_b):
    b, s_len, d = x_prompt.shape
    bd, t_new, _ = x_sample.shape
    depth = w_in.shape[0]
    p = cache_fox_k.shape[2]
    n = N_META + s_len
    alpha = (2 * depth) ** 0.25
    fox_w = FOX_HEADS * HEAD_DIM
    diff_w = DIFF_HEADS * 2 * HEAD_DIM

    o_f = 3 * fox_w
    o_b = o_f + FOX_HEADS
    o_g = o_b + 3 * diff_w
    w_in_b = w_in.astype(BF16)
    w_fox = w_in_b[:, :, :o_f]
    w_f = jnp.pad(w_in_b[:, :, o_f:o_b], ((0, 0), (0, 0), (0, LANES - FOX_HEADS)))
    w_diff = w_in_b[:, :, o_b:o_g]
    w_ga = w_in_b[:, :, o_g:o_g + d]
    w_gb = w_in_b[:, :, o_g + d:]
    b_f_pad = jnp.pad(b_f.astype(F32), ((0, 0), (0, LANES - FOX_HEADS)))[:, None, :]
    w_a, w_b, w_o = w_br_a.astype(BF16), w_br_b.astype(BF16), w_out.astype(BF16)
    w_u, w_d = w_up.astype(BF16), w_down.astype(BF16)

    cache_fk = cache_fox_k.reshape(depth, bd, p, fox_w)
    cache_fv = cache_fox_v.reshape(depth, bd, p, fox_w)
    cache_dk = cache_diff_k.reshape(depth, bd, p, diff_w)
    cache_dv = cache_diff_v.reshape(depth, bd, p, diff_w)

    cs_p = jnp.tile(_rope_table(jnp.arange(n)), (b, 1))
    cs_s = jnp.tile(_rope_table(p + jnp.arange(t_new)), (bd, 1))

    meta = jnp.broadcast_to(meta_tokens[None].astype(x_prompt.dtype), (b, N_META, d))
    h = _input_layer_norm(jnp.concatenate([meta, x_prompt], axis=1).reshape(b * n, d), ln_in_g, ln_in_b)
    s = _input_layer_norm(x_sample.reshape(bd * t_new, d), ln_in_g, ln_in_b)

    outs_p = [[] for _ in range(5)]
    outs_s = [[] for _ in range(5)]
    for l in range(depth):
        lam_init = 0.8 - 0.6 * math.exp(-0.3 * l)
        lam_params = [a[l].reshape(1, HEAD_DIM).astype(F32) for a in (lambda_q1, lambda_k1, lambda_q2, lambda_k2)]
        g_sub = subln_g[l].reshape(1, 2 * HEAD_DIM).astype(F32)

        def channel_mix(x, oa, ob):
            m = _merge(x, oa, ob, w_ga[l], w_gb[l], w_a[l], w_b[l])
            x = _out_proj_ln(x, m, w_o[l], ln1_g[l], ln1_b[l], alpha)
            return _mlp_ln(x, w_u[l], w_d[l], ln2_g[l], ln2_b[l], alpha)

        qa, ka, va, lf = _project_fox(h, w_fox[l], w_f[l], b_f_pad[l])
        qb, kb, vb = _project_diff(h, w_diff[l], cs_p)
        lf = lf.reshape(b, n, FOX_HEADS)
        c = _cumulative_logf(lf)
        shape3 = lambda a: a.reshape(b, n, a.shape[-1])
        oa = _fox_prompt(shape3(qa), shape3(ka), shape3(va), c)
        ob = _diff_prompt(shape3(qb), shape3(kb), shape3(vb), lam_params, g_sub, lam_init)
        for dst, a in zip(outs_p, (ka.reshape(b, n, FOX_HEADS, HEAD_DIM), va.reshape(b, n, FOX_HEADS, HEAD_DIM), lf,
                                   kb.reshape(b, n, DIFF_HEADS, 2 * HEAD_DIM), vb.reshape(b, n, DIFF_HEADS, 2 * HEAD_DIM))):
            dst.append(a)
        h = channel_mix(h, oa.reshape(b * n, fox_w), ob.reshape(b * n, diff_w))

        qa2, ka2, va2, lf2 = _project_fox(s, w_fox[l], w_f[l], b_f_pad[l])
        qb2, kb2, vb2 = _project_diff(s, w_diff[l], cs_s)
        lf2 = lf2.reshape(bd, t_new, FOX_HEADS)
        c_all = _cumulative_logf(jnp.concatenate([cache_fox_logf[l].astype(F32), lf2], axis=1))
        new3 = lambda a: a.reshape(bd, t_new, a.shape[-1])
        oa2 = _fox_sample(new3(qa2), cache_fk, cache_fv, l, new3(ka2), new3(va2), c_all)
        ob2 = _diff_sample(new3(qb2), cache_dk, cache_dv, l, new3(kb2), new3(vb2), lam_params, g_sub, lam_init)
        for dst, a in zip(outs_s, (ka2.reshape(bd, t_new, FOX_HEADS, HEAD_DIM), va2.reshape(bd, t_new, FOX_HEADS, HEAD_DIM), lf2,
                                   kb2.reshape(bd, t_new, DIFF_HEADS, 2 * HEAD_DIM),
                                   vb2.reshape(bd, t_new, DIFF_HEADS, 2 * HEAD_DIM))):
            dst.append(a)
        s = channel_mix(s, oa2.reshape(bd * t_new, fox_w), ob2.reshape(bd * t_new, diff_w))

    y_prompt = h.reshape(b, n, d)[:, N_META:]
    y_sample = s.reshape(bd, t_new, d)
    return (y_prompt, y_sample, *(jnp.stack(o) for o in outs_p), *(jnp.stack(o) for o in outs_s))
```

```python
import functools
import math

import jax
import jax.numpy as jnp
from jax import lax
from jax.experimental import pallas as pl
from jax.experimental.pallas import tpu as pltpu

N_META = 16
CHUNK = 64
FOX_HEADS = 8
DIFF_HEADS = 4
HEAD_DIM = 128
ROT_DIM = 32
ROPE_THETA = 500000.0
LN_EPS = 1e-5
RMS_EPS = 1e-5
NEG = -1e30

LANES = 128
VMEM_LIMIT_BYTES = 56 * 2**20

ROW_TILE = 512
ATT_TILE = 512
FF_TILE = 1024
COL_CHUNK = 512
ATT_STRIP = 256
ATT_STRIPS = 8
ATT_AHEAD = 6
ONES_ROWS = 16
BIAS_PARTS = 3
LOG2E = math.log2(math.e)

F32 = jnp.float32
BF16 = jnp.bfloat16


def _params(*sem):
    return pltpu.CompilerParams(dimension_semantics=sem, vmem_limit_bytes=VMEM_LIMIT_BYTES)


def _resident(shape):
    return pl.BlockSpec(shape, lambda *_: (0,) * len(shape), pipeline_mode=pl.Buffered(1))


def _layer_norm(x, g, b):
    mu = jnp.mean(x, axis=-1, keepdims=True)
    xc = x - mu
    var = jnp.mean(xc * xc, axis=-1, keepdims=True)
    return xc * lax.rsqrt(var + LN_EPS) * g + b


def _dot(a, b):
    return jnp.dot(a, b, preferred_element_type=F32)


def _dot_nt(a, b):
    return lax.dot_general(a, b, (((1,), (1,)), ((), ())), preferred_element_type=F32)


def _ln_kernel(x_ref, g_ref, b_ref, o_ref):
    o_ref[...] = _layer_norm(x_ref[...], g_ref[...], b_ref[...])


def _input_layer_norm(x, g, b):
    t, d = x.shape
    tm = min(ROW_TILE, t)
    row = pl.BlockSpec((tm, d), lambda i: (i, 0))
    vec = pl.BlockSpec((1, d), lambda i: (0, 0))
    return pl.pallas_call(
        _ln_kernel, grid=(pl.cdiv(t, tm),), in_specs=[row, vec, vec], out_specs=row,
        out_shape=jax.ShapeDtypeStruct((t, d), F32), compiler_params=_params("parallel"),
        name="input_ln")(x, g.reshape(1, d), b.reshape(1, d))


def _proj_fox_kernel(h_ref, w_ref, wf_ref, bf_ref, q_ref, k_ref, v_ref, lf_ref):
    hb = h_ref[...].astype(BF16)
    w = q_ref.shape[1]
    q_ref[...] = _dot(hb, w_ref[:, 0:w]).astype(BF16)
    k_ref[...] = _dot(hb, w_ref[:, w:2 * w])
    v_ref[...] = _dot(hb, w_ref[:, 2 * w:3 * w])
    fa = _dot(hb, wf_ref[...]) + bf_ref[...]
    lf = jnp.minimum(fa, 0.0) - jnp.log1p(jnp.exp(-jnp.abs(fa)))
    lf_ref[...] = lf[:, :lf_ref.shape[1]]


def _project_fox(h, w, wf, bf):
    t, d = h.shape
    width = w.shape[1] // 3
    tm = min(ROW_TILE, t)
    rows = lambda n: pl.BlockSpec((tm, n), lambda i: (i, 0))
    return pl.pallas_call(
        _proj_fox_kernel, grid=(pl.cdiv(t, tm),),
        in_specs=[rows(d), _resident(w.shape), _resident(wf.shape), _resident(bf.shape)],
        out_specs=[rows(width), rows(width), rows(width), rows(FOX_HEADS)],
        out_shape=[jax.ShapeDtypeStruct((t, width), BF16), jax.ShapeDtypeStruct((t, width), F32),
                   jax.ShapeDtypeStruct((t, width), F32), jax.ShapeDtypeStruct((t, FOX_HEADS), F32)],
        compiler_params=_params("parallel"), name="proj_fox")(h, w, wf, bf)


def _proj_diff_kernel(h_ref, w_ref, cs_ref, q_ref, k_ref, v_ref):
    hb = h_ref[...].astype(BF16)
    tm, w = q_ref.shape
    cos = cs_ref[:, 0:LANES]
    sin = cs_ref[:, LANES:2 * LANES]
    half = ROT_DIM // 2
    first = lax.broadcasted_iota(jnp.int32, (tm, LANES), 1) < half

    def rope(x):
        partner = jnp.where(first, pltpu.roll(x, LANES - half, 1), pltpu.roll(x, half, 1))
        return x * cos + partner * sin

    for part, ref in enumerate((q_ref, k_ref)):
        z = _dot(hb, w_ref[:, part * w:(part + 1) * w])
        for g in range(w // LANES):
            sl = slice(g * LANES, (g + 1) * LANES)
            ref[:, sl] = rope(z[:, sl]).astype(ref.dtype)
    v_ref[...] = _dot(hb, w_ref[:, 2 * w:3 * w])


def _project_diff(h, w, cs):
    t, d = h.shape
    width = w.shape[1] // 3
    tm = min(ROW_TILE, t)
    rows = lambda n: pl.BlockSpec((tm, n), lambda i: (i, 0))
    return pl.pallas_call(
        _proj_diff_kernel, grid=(pl.cdiv(t, tm),),
        in_specs=[rows(d), _resident(w.shape), rows(2 * LANES)],
        out_specs=[rows(width), rows(width), rows(width)],
        out_shape=[jax.ShapeDtypeStruct((t, width), BF16), jax.ShapeDtypeStruct((t, width), F32),
                   jax.ShapeDtypeStruct((t, width), F32)],
        compiler_params=_params("parallel"), name="proj_diff")(h, w, cs)


def _rope_table(pos):
    half = ROT_DIM // 2
    inv_freq = ROPE_THETA ** (-jnp.arange(half, dtype=F32) / half)
    ang = pos.astype(F32)[:, None] * inv_freq[None, :]
    cos, sin = jnp.cos(ang), jnp.sin(ang)
    n = pos.shape[0]
    ones = jnp.ones((n, LANES - ROT_DIM), F32)
    zeros = jnp.zeros((n, LANES - ROT_DIM), F32)
    return jnp.concatenate([cos, cos, ones, -sin, sin, zeros], axis=1)


def _cumsum_kernel(x_ref, o_ref):
    rows, n = x_ref.shape
    lane = lax.broadcasted_iota(jnp.int32, (rows, LANES), 1)
    carry = jnp.zeros((rows, 1), F32)
    for j in range(n // LANES):
        sl = slice(j * LANES, (j + 1) * LANES)
        v = x_ref[:, sl]
        shift = 1
        while shift < LANES:
            v = v + jnp.where(lane >= shift, pltpu.roll(v, shift, 1), 0.0)
            shift *= 2
        v = v + carry
        o_ref[:, sl] = v
        carry = v[:, LANES - 1:LANES]


def _cumsum_lanes(x):
    b, r, n = x.shape
    spec = pl.BlockSpec((None, r, n), lambda i: (i, 0, 0))
    return pl.pallas_call(
        _cumsum_kernel, grid=(b,), in_specs=[spec], out_specs=spec,
        out_shape=jax.ShapeDtypeStruct(x.shape, F32), compiler_params=_params("parallel"),
        name="logf_cumsum")(x)


def _cumulative_logf(logf):
    b, n, r = logf.shape
    n_pad = pl.cdiv(n, LANES) * LANES
    x = jnp.pad(jnp.transpose(logf, (0, 2, 1)), ((0, 0), (0, 0), (0, n_pad - n)))
    return _cumsum_lanes(x)[:, :, :n]


def _split_bf16(x, parts):
    out, rest = [], x
    for _ in range(parts):
        t = rest.astype(BF16)
        out.append(t)
        rest = rest - t.astype(F32)
    return out


def _fox_bias_kernel(lf_ref, a_ref, *, inv_scale):
    n = lf_ref.shape[0]
    row = lax.broadcasted_iota(jnp.int32, (LANES, LANES), 0)
    col = lax.broadcasted_iota(jnp.int32, (LANES, LANES), 1)
    tri = jnp.where(col <= row, 1.0, 0.0).astype(BF16)
    spread = [jnp.where(col == BIAS_PARTS * row + t, 1.0, 0.0).astype(BF16) for t in range(BIAS_PARTS)]
    carry = jnp.zeros((1, LANES), F32)
    for r0 in range(0, n, LANES):
        rows = min(LANES, n - r0)
        x = lf_ref[r0:r0 + rows]
        cs = carry
        for term in _split_bf16(x, BIAS_PARTS):
            cs = cs + _dot(tri[:rows, :rows], term)
        carry = cs[rows - 1:rows]
        a = cs * (-inv_scale)
        out = jnp.zeros((rows, LANES), F32)
        for term, e in zip(_split_bf16(a, BIAS_PARTS), spread):
            out = out + _dot(term, e)
        a_ref[r0:r0 + rows] = out.astype(BF16)


def _fox_bias(logf):
    b, n, h = logf.shape
    x = jnp.pad(logf, ((0, 0), (0, 0), (0, LANES - h)))
    spec = pl.BlockSpec((None, n, LANES), lambda i: (i, 0, 0))
    return pl.pallas_call(
        functools.partial(_fox_bias_kernel, inv_scale=HEAD_DIM ** 0.5), grid=(b,), in_specs=[spec], out_specs=spec,
        out_shape=jax.ShapeDtypeStruct((b, n, LANES), BF16), compiler_params=_params("parallel"),
        name="fox_bias")(x)


def _attend(stages, c_exp):
    pending = {}
    for t in range(min(ATT_AHEAD, len(stages))):
        pending[t] = stages[t][0]()
    for t, (_, mask, (m_ref, acc_ref), slot, w, vt_blk) in enumerate(stages):
        if t + ATT_AHEAD < len(stages):
            pending[t + ATT_AHEAD] = stages[t + ATT_AHEAD][0]()
        st = pending.pop(t)
        if mask is not None:
            st = jnp.where(mask, st, NEG)
        m_prev = m_ref[slot, :, 0:w]
        m_new = jnp.maximum(m_prev, jnp.max(st, axis=0, keepdims=True))
        alpha = jnp.exp2((m_prev - m_new) * c_exp)
        p = jnp.exp2((st - m_new) * c_exp)
        m_ref[slot, :, 0:w] = m_new
        acc_ref[slot, :, 0:w] = alpha * acc_ref[slot, :, 0:w] + _dot(vt_blk, p.astype(BF16))


def _reset(state, slot, w):
    m_ref, acc_ref = state
    m_ref[slot, :, 0:w] = jnp.full((1, w), NEG, F32)
    acc_ref[slot, :, 0:w] = jnp.zeros((acc_ref.shape[1], w), F32)


def _normalised(state, slot, w):
    m_ref, acc_ref = state
    dv = acc_ref.shape[1] - ONES_ROWS
    return acc_ref[slot, 0:dv, 0:w] * (1.0 / acc_ref[slot, dv:dv + 1, 0:w])


def _state_scratch(slots, dv):
    return [pltpu.VMEM((slots, 1, ATT_STRIP), F32), pltpu.VMEM((slots, dv + ONES_ROWS, ATT_STRIP), F32)]


def _transpose_values(v_ref, vt, vtm, s_len):
    dv = v_ref.shape[1]
    sw = ATT_STRIP
    vt[:, dv:, :] = jnp.ones((vt.shape[0], ONES_ROWS, sw), BF16)
    vtm[dv:, :] = jnp.ones((ONES_ROWS, LANES), BF16)
    for j in range(s_len // sw):
        for c in range(sw // LANES):
            r0 = N_META + j * sw + c * LANES
            for e in range(dv // LANES):
                vt[j, e * LANES:(e + 1) * LANES, c * LANES:(c + 1) * LANES] = (
                    v_ref[r0:r0 + LANES, e * LANES:(e + 1) * LANES].T.astype(BF16))
    vmeta = jnp.concatenate([v_ref[0:N_META], jnp.zeros((LANES - N_META, dv), F32)], axis=0)
    for e in range(dv // LANES):
        vtm[e * LANES:(e + 1) * LANES] = vmeta[:, e * LANES:(e + 1) * LANES].T.astype(BF16)


class _Softmax:
    def __init__(self, m_ref, l_ref, acc_ref, rows):
        self.m, self.l, self.acc, self.rows = m_ref, l_ref, acc_ref, rows

    def reset(self):
        r = self.rows
        self.m[0:r] = jnp.full((r, 1), NEG, F32)
        self.l[0:r] = jnp.zeros((r, 1), F32)
        self.acc[0:r] = jnp.zeros((r, self.acc.shape[1]), F32)

    def update(self, z, v):
        r = self.rows
        m_prev = self.m[0:r]
        m_new = jnp.maximum(m_prev, jnp.max(z, axis=1, keepdims=True))
        scale = jnp.exp(m_prev - m_new)
        p = jnp.exp(z - m_new)
        self.l[0:r] = scale * self.l[0:r] + jnp.sum(p, axis=1, keepdims=True)
        self.acc[0:r] = scale * self.acc[0:r] + _dot(p.astype(BF16), v)
        self.m[0:r] = m_new

    def result(self):
        r = self.rows
        return self.acc[0:r] / self.l[0:r]


def _softmax_scratch(rows, dv):
    return [pltpu.VMEM((rows, 1), F32), pltpu.VMEM((rows, 1), F32), pltpu.VMEM((rows, dv), F32)]


def _fox_prompt_kernel(q_ref, k_ref, v_ref, a_ref, o_ref, qa, ka, vt, vtm, m_ref, acc_ref, *, ns):
    n = q_ref.shape[0]
    s_len = n - N_META
    sw = ATT_STRIP
    tq = ns * sw
    d = HEAD_DIM
    c_exp = d ** -0.5 * LOG2E
    state = (m_ref, acc_ref)
    lane = lax.broadcasted_iota(jnp.int32, (1, LANES), 1)
    first = pl.program_id(1) * BIAS_PARTS
    onehot = jnp.where((lane >= first) & (lane < first + BIAS_PARTS), 1.0, 0.0).astype(BF16)
    qa[:, 0:d] = q_ref[...]
    qa[:, d:] = jnp.broadcast_to(onehot, (n, LANES))
    ka[0:s_len, 0:d] = k_ref[N_META:n].astype(BF16)
    ka[0:s_len, d:] = a_ref[N_META:n]
    ka[s_len:s_len + N_META, 0:d] = k_ref[0:N_META].astype(BF16)
    ka[s_len:s_len + N_META, d:] = a_ref[0:N_META]
    ka[s_len + N_META:, :] = jnp.zeros((LANES - N_META, 2 * d), BF16)
    _transpose_values(v_ref, vt, vtm, s_len)
    k_meta = ka[s_len:s_len + LANES]

    kidx = lax.broadcasted_iota(jnp.int32, (LANES, LANES), 0)
    qidx = lax.broadcasted_iota(jnp.int32, (LANES, LANES), 1)
    _reset(state, 0, LANES)
    _attend([(lambda: _dot_nt(k_meta, qa[0:LANES]), (kidx <= qidx) & (kidx < N_META), state, 0, LANES, vtm[...])],
            c_exp)
    o_ref[0:N_META] = _normalised(state, 0, LANES).T[0:N_META].astype(o_ref.dtype)

    meta_valid = lax.broadcasted_iota(jnp.int32, (LANES, sw), 0) < N_META
    diag = lax.broadcasted_iota(jnp.int32, (sw, sw), 0) <= lax.broadcasted_iota(jnp.int32, (sw, sw), 1)

    def q_block(i, carry):
        r0 = pl.multiple_of(i * tq, tq)

        def scores(k_blk, c):
            return lambda: _dot_nt(k_blk, qa[pl.ds(N_META + r0 + c * sw, sw)])

        for c in range(ns):
            _reset(state, c, sw)
        _attend([(scores(k_meta, c), meta_valid, state, c, sw, vtm[...]) for c in range(ns)], c_exp)

        def kv_block(j, cc):
            k_blk = ka[pl.ds(pl.multiple_of(j * sw, sw), sw)]
            _attend([(scores(k_blk, c), None, state, c, sw, vt[j]) for c in range(ns)], c_exp)
            return cc

        lax.fori_loop(0, i * ns, kv_block, 0)
        stages = []
        for dd in range(ns):
            k_blk = ka[pl.ds(r0 + dd * sw, sw)]
            stages += [(scores(k_blk, c), diag if c == dd else None, state, c, sw, vt[i * ns + dd])
                       for c in range(dd, ns)]
        _attend(stages, c_exp)
        for c in range(ns):
            o_ref[pl.ds(N_META + r0 + c * sw, sw)] = _normalised(state, c, sw).T.astype(o_ref.dtype)
        return carry

    lax.fori_loop(0, s_len // tq, q_block, 0)


def _fox_prompt(q, k, v, a):
    b, n, width = q.shape
    d = HEAD_DIM
    s_len = n - N_META
    ns = min(ATT_STRIPS, s_len // ATT_STRIP)
    head = pl.BlockSpec((None, n, d), lambda i, h: (i, 0, h))
    return pl.pallas_call(
        functools.partial(_fox_prompt_kernel, ns=ns), grid=(b, width // d),
        in_specs=[head, head, head, pl.BlockSpec((None, n, LANES), lambda i, h: (i, 0, 0))],
        out_specs=head, out_shape=jax.ShapeDtypeStruct(q.shape, BF16),
        scratch_shapes=[pltpu.VMEM((n, 2 * d), BF16), pltpu.VMEM((s_len + LANES, 2 * d), BF16),
                        pltpu.VMEM((s_len // ATT_STRIP, d + ONES_ROWS, ATT_STRIP), BF16),
                        pltpu.VMEM((d + ONES_ROWS, LANES), BF16)]
        + _state_scratch(ns, d),
        compiler_params=_params("parallel", "parallel"), name="fox_prompt")(q, k, v, a)


def _fox_sample_kernel(q_ref, kc_ref, vc_ref, kn_ref, vn_ref, cc_ref, cn_ref, o_ref, m_ref, l_ref, acc_ref, *, tk):
    t = q_ref.shape[0]
    scale = HEAD_DIM ** -0.5
    q = q_ref[...]
    sm = _Softmax(m_ref, l_ref, acc_ref, t)
    sm.reset()

    def cache_block(j, c):
        c0 = pl.multiple_of(j * tk, tk)
        z = _dot_nt(q, kc_ref[pl.ds(c0, tk)].astype(BF16)) * scale - cc_ref[pl.ds(j, 1)]
        sm.update(z, vc_ref[pl.ds(c0, tk)].astype(BF16))
        return c

    lax.fori_loop(0, kc_ref.shape[0] // tk, cache_block, 0)
    z = _dot_nt(q, kn_ref[...].astype(BF16)) * scale - cn_ref[...]
    mask = lax.broadcasted_iota(jnp.int32, (t, t), 0) >= lax.broadcasted_iota(jnp.int32, (t, t), 1)
    sm.update(jnp.where(mask, z, NEG), vn_ref[...].astype(BF16))
    o_ref[...] = sm.result().astype(o_ref.dtype)


def _fox_sample(q, cache_k, cache_v, layer, k_new, v_new, c):
    b, t, width = q.shape
    heads = width // HEAD_DIM
    p = cache_k.shape[2]
    tk = min(ATT_TILE, p)
    c_cache = c[:, :, :p].reshape(b, heads, p // tk, tk)
    c_new = c[:, :, None, p:]
    new = pl.BlockSpec((None, t, HEAD_DIM), lambda i, h: (i, 0, h))
    cache = pl.BlockSpec((None, None, p, HEAD_DIM), lambda i, h: (layer, i, 0, h))
    return pl.pallas_call(
        functools.partial(_fox_sample_kernel, tk=tk), grid=(b, heads),
        in_specs=[new, cache, cache, new, new,
                  pl.BlockSpec((None, None, p // tk, tk), lambda i, h: (i, h, 0, 0)),
                  pl.BlockSpec((None, None, 1, t), lambda i, h: (i, h, 0, 0))],
        out_specs=new, out_shape=jax.ShapeDtypeStruct(q.shape, BF16),
        scratch_shapes=_softmax_scratch(t, HEAD_DIM),
        compiler_params=_params("parallel", "parallel"), name="fox_sample")(
            q, cache_k, cache_v, k_new, v_new, c_cache, c_new)


def _diff_lambda(lq1_ref, lk1_ref, lq2_ref, lk2_ref, lam_init):
    d1 = jnp.sum(lq1_ref[...] * lk1_ref[...], axis=1, keepdims=True)
    d2 = jnp.sum(lq2_ref[...] * lk2_ref[...], axis=1, keepdims=True)
    return jnp.exp(d1) - jnp.exp(d2) + lam_init


def _diff_finish(s1, s2, lam, g_ref, lam_init):
    o = s1.result() - lam * s2.result()
    o = o * lax.rsqrt(jnp.mean(o * o, axis=1, keepdims=True) + RMS_EPS) * g_ref[...]
    return o * (1.0 - lam_init)


def _diff_finish_t(s1, s2, slot, w, lam, g_ref, lam_init):
    ot = _normalised(s1, slot, w) - lam * _normalised(s2, slot, w)
    ot = ot * lax.rsqrt(jnp.mean(ot * ot, axis=0, keepdims=True) + RMS_EPS)
    return ot.T * g_ref[...] * (1.0 - lam_init)


def _diff_prompt_kernel(q_ref, k_ref, v_ref, lq1_ref, lk1_ref, lq2_ref, lk2_ref, g_ref, o_ref,
                        kb, vt, vtm, m1, a1, m2, a2, *, ns, lam_init):
    n = q_ref.shape[0]
    s_len = n - N_META
    sw = ATT_STRIP
    tq = ns * sw
    d = HEAD_DIM
    c_exp = d ** -0.5 * LOG2E
    states = ((m1, a1), (m2, a2))
    lam = _diff_lambda(lq1_ref, lk1_ref, lq2_ref, lk2_ref, lam_init)
    kb[0:s_len] = k_ref[N_META:n].astype(BF16)
    kb[s_len:s_len + N_META] = k_ref[0:N_META].astype(BF16)
    kb[s_len + N_META:] = jnp.zeros((LANES - N_META, 2 * d), BF16)
    _transpose_values(v_ref, vt, vtm, s_len)

    def both(k_rows, q_rows, c, w, mask, vt_blk):
        return [(functools.partial(lambda lo: _dot_nt(kb[k_rows, lo:lo + d], q_ref[q_rows, lo:lo + d]), e * d),
                 mask, states[e], c, w, vt_blk) for e in range(2)]

    meta_rows = pl.ds(s_len, LANES)
    for st in states:
        _reset(st, 0, LANES)
    meta_keys = lax.broadcasted_iota(jnp.int32, (LANES, LANES), 0) < N_META
    _attend(both(meta_rows, pl.ds(0, LANES), 0, LANES, meta_keys, vtm[...]), c_exp)
    o_ref[0:N_META] = _diff_finish_t(states[0], states[1], 0, LANES, lam, g_ref, lam_init)[0:N_META].astype(o_ref.dtype)

    meta_valid = lax.broadcasted_iota(jnp.int32, (LANES, sw), 0) < N_META
    diag = (lax.broadcasted_iota(jnp.int32, (sw, sw), 0) // CHUNK
            <= lax.broadcasted_iota(jnp.int32, (sw, sw), 1) // CHUNK)

    def q_block(i, carry):
        r0 = pl.multiple_of(i * tq, tq)

        def q_rows(c):
            return pl.ds(N_META + r0 + c * sw, sw)

        for c in range(ns):
            for st in states:
                _reset(st, c, sw)
        stages = []
        for c in range(ns):
            stages += both(meta_rows, q_rows(c), c, sw, meta_valid, vtm[...])
        _attend(stages, c_exp)

        def kv_block(j, cc):
            k_rows = pl.ds(pl.multiple_of(j * sw, sw), sw)
            stages = []
            for c in range(ns):
                stages += both(k_rows, q_rows(c), c, sw, None, vt[j])
            _attend(stages, c_exp)
            return cc

        lax.fori_loop(0, i * ns, kv_block, 0)
        stages = []
        for dd in range(ns):
            for c in range(dd, ns):
                stages += both(pl.ds(r0 + dd * sw, sw), q_rows(c), c, sw, diag if c == dd else None, vt[i * ns + dd])
        _attend(stages, c_exp)
        for c in range(ns):
            o_ref[q_rows(c)] = _diff_finish_t(states[0], states[1], c, sw, lam, g_ref, lam_init).astype(o_ref.dtype)
        return carry

    lax.fori_loop(0, s_len // tq, q_block, 0)


def _lambda_specs():
    return [pl.BlockSpec((1, HEAD_DIM), lambda i, h: (0, 0))] * 4 + [
        pl.BlockSpec((1, 2 * HEAD_DIM), lambda i, h: (0, 0))]


def _diff_prompt(q, k, v, lam_params, g, lam_init):
    b, n, width = q.shape
    dv = 2 * HEAD_DIM
    s_len = n - N_META
    ns = min(ATT_STRIPS, s_len // ATT_STRIP)
    head = pl.BlockSpec((None, n, dv), lambda i, h: (i, 0, h))
    return pl.pallas_call(
        functools.partial(_diff_prompt_kernel, ns=ns, lam_init=lam_init), grid=(b, width // dv),
        in_specs=[head, head, head] + _lambda_specs(),
        out_specs=head, out_shape=jax.ShapeDtypeStruct(q.shape, BF16),
        scratch_shapes=[pltpu.VMEM((s_len + LANES, dv), BF16),
                        pltpu.VMEM((s_len // ATT_STRIP, dv + ONES_ROWS, ATT_STRIP), BF16),
                        pltpu.VMEM((dv + ONES_ROWS, LANES), BF16)]
        + _state_scratch(ns, dv) + _state_scratch(ns, dv),
        compiler_params=_params("parallel", "parallel"), name="diff_prompt")(q, k, v, *lam_params, g)


def _diff_sample_kernel(q_ref, kc_ref, vc_ref, kn_ref, vn_ref, lq1_ref, lk1_ref, lq2_ref, lk2_ref, g_ref, o_ref,
                        m1, l1, a1, m2, l2, a2, *, tk, lam_init):
    t = q_ref.shape[0]
    d = HEAD_DIM
    scale = d ** -0.5
    lam = _diff_lambda(lq1_ref, lk1_ref, lq2_ref, lk2_ref, lam_init)
    q = q_ref[...]
    s1, s2 = _Softmax(m1, l1, a1, t), _Softmax(m2, l2, a2, t)
    s1.reset()
    s2.reset()

    def both(k, v):
        for sm, lo in ((s1, 0), (s2, d)):
            sm.update(_dot_nt(q[:, lo:lo + d], k[:, lo:lo + d]) * scale, v)

    def cache_block(j, c):
        c0 = pl.multiple_of(j * tk, tk)
        both(kc_ref[pl.ds(c0, tk)].astype(BF16), vc_ref[pl.ds(c0, tk)].astype(BF16))
        return c

    lax.fori_loop(0, kc_ref.shape[0] // tk, cache_block, 0)
    both(kn_ref[...].astype(BF16), vn_ref[...].astype(BF16))
    o_ref[...] = _diff_finish(s1, s2, lam, g_ref, lam_init).astype(o_ref.dtype)


def _diff_sample(q, cache_k, cache_v, layer, k_new, v_new, lam_params, g, lam_init):
    b, t, width = q.shape
    dv = 2 * HEAD_DIM
    heads = width // dv
    p = cache_k.shape[2]
    tk = min(ATT_TILE, p)
    new = pl.BlockSpec((None, t, dv), lambda i, h: (i, 0, h))
    cache = pl.BlockSpec((None, None, p, dv), lambda i, h: (layer, i, 0, h))
    return pl.pallas_call(
        functools.partial(_diff_sample_kernel, tk=tk, lam_init=lam_init), grid=(b, heads),
        in_specs=[new, cache, cache, new, new] + _lambda_specs(),
        out_specs=new, out_shape=jax.ShapeDtypeStruct(q.shape, BF16),
        scratch_shapes=_softmax_scratch(t, dv) + _softmax_scratch(t, dv),
        compiler_params=_params("parallel", "parallel"), name="diff_sample")(
            q, cache_k, cache_v, k_new, v_new, *lam_params, g)


def _merge_kernel(h_ref, oa_ref, ob_ref, wga_ref, wgb_ref, wa_ref, wb_ref, m_ref):
    hb = h_ref[...].astype(BF16)
    oa, ob = oa_ref[...], ob_ref[...]
    for c in range(m_ref.shape[1] // COL_CHUNK):
        sl = slice(c * COL_CHUNK, (c + 1) * COL_CHUNK)
        ga = jax.nn.sigmoid(_dot(hb, wga_ref[:, sl]))
        gb = jax.nn.sigmoid(_dot(hb, wgb_ref[:, sl]))
        m_ref[:, sl] = (ga * _dot(oa, wa_ref[:, sl]) + gb * _dot(ob, wb_ref[:, sl])).astype(m_ref.dtype)


def _merge(h, oa, ob, wga, wgb, wa, wb):
    t, d = h.shape
    tm = min(ROW_TILE, t)
    rows = lambda n: pl.BlockSpec((tm, n), lambda i: (i, 0))
    return pl.pallas_call(
        _merge_kernel, grid=(pl.cdiv(t, tm),),
        in_specs=[rows(d), rows(oa.shape[1]), rows(ob.shape[1]),
                  _resident(wga.shape), _resident(wgb.shape), _resident(wa.shape), _resident(wb.shape)],
        out_specs=rows(d), out_shape=jax.ShapeDtypeStruct((t, d), BF16),
        compiler_params=_params("parallel"), name="gated_merge")(h, oa, ob, wga, wgb, wa, wb)


def _out_ln_kernel(h_ref, m_ref, w_ref, g_ref, b_ref, o_ref, *, alpha):
    y = _dot(m_ref[...], w_ref[...])
    o_ref[...] = _layer_norm(alpha * h_ref[...] + y, g_ref[...], b_ref[...])


def _out_proj_ln(h, m, w, g, b, alpha):
    t, d = h.shape
    tm = min(ROW_TILE, t)
    row = pl.BlockSpec((tm, d), lambda i: (i, 0))
    vec = pl.BlockSpec((1, d), lambda i: (0, 0))
    return pl.pallas_call(
        functools.partial(_out_ln_kernel, alpha=alpha), grid=(pl.cdiv(t, tm),),
        in_specs=[row, row, _resident(w.shape), vec, vec], out_specs=row,
        out_shape=jax.ShapeDtypeStruct((t, d), F32),
        compiler_params=_params("parallel"), name="out_proj_ln")(h, m, w, g.reshape(1, d), b.reshape(1, d))


def _mlp_kernel(x_ref, wu_ref, wd_ref, g_ref, b_ref, o_ref, xb_ref, acc_ref, *, alpha):
    f = pl.program_id(1)

    @pl.when(f == 0)
    def _():
        xb_ref[...] = x_ref[...].astype(BF16)
        acc_ref[...] = jnp.zeros(acc_ref.shape, F32)

    u = jnp.maximum(_dot(xb_ref[...], wu_ref[...]), 0.0)
    acc_ref[...] += _dot((u * u).astype(BF16), wd_ref[...])

    @pl.when(f == pl.num_programs(1) - 1)
    def _():
        o_ref[...] = _layer_norm(alpha * x_ref[...] + acc_ref[...], g_ref[...], b_ref[...])


def _mlp_ln(x, wu, wd, g, b, alpha):
    t, d = x.shape
    dff = wu.shape[1]
    tm = min(ROW_TILE, t)
    tf = min(FF_TILE, dff)
    row = pl.BlockSpec((tm, d), lambda i, f: (i, 0))
    vec = pl.BlockSpec((1, d), lambda i, f: (0, 0))
    return pl.pallas_call(
        functools.partial(_mlp_kernel, alpha=alpha), grid=(pl.cdiv(t, tm), dff // tf),
        in_specs=[row, pl.BlockSpec((d, tf), lambda i, f: (0, f)), pl.BlockSpec((tf, d), lambda i, f: (f, 0)),
                  vec, vec],
        out_specs=row, out_shape=jax.ShapeDtypeStruct((t, d), F32),
        scratch_shapes=[pltpu.VMEM((tm, d), BF16), pltpu.VMEM((tm, d), F32)],
        compiler_params=_params("parallel", "arbitrary"), name="mlp_ln")(
            x, wu, wd, g.reshape(1, d), b.reshape(1, d))


def kernel(x_prompt, x_sample, cache_fox_k, cache_fox_v, cache_fox_logf, cache_diff_k, cache_diff_v, meta_tokens, ln_in_g, ln_in_b, w_in, b_f, lambda_q1, lambda_k1, lambda_q2, lambda_k2, subln_g, w_br_a, w_br_b, w_out, ln1_g, ln1_b, w_up, w_down, ln2_g, ln2_b):
    b, s_len, d = x_prompt.shape
    bd, t_new, _ = x_sample.shape
    depth = w_in.shape[0]
    p = cache_fox_k.shape[2]
    n = N_META + s_len
    alpha = (2 * depth) ** 0.25
    fox_w = FOX_HEADS * HEAD_DIM
    diff_w = DIFF_HEADS * 2 * HEAD_DIM

    o_f = 3 * fox_w
    o_b = o_f + FOX_HEADS
    o_g = o_b + 3 * diff_w
    w_in_b = w_in.astype(BF16)
    w_fox = w_in_b[:, :, :o_f]
    w_f = jnp.pad(w_in_b[:, :, o_f:o_b], ((0, 0), (0, 0), (0, LANES - FOX_HEADS)))
    w_diff = w_in_b[:, :, o_b:o_g]
    w_ga = w_in_b[:, :, o_g:o_g + d]
    w_gb = w_in_b[:, :, o_g + d:]
    b_f_pad = jnp.pad(b_f.astype(F32), ((0, 0), (0, LANES - FOX_HEADS)))[:, None, :]
    w_a, w_b, w_o = w_br_a.astype(BF16), w_br_b.astype(BF16), w_out.astype(BF16)
    w_u, w_d = w_up.astype(BF16), w_down.astype(BF16)

    cache_fk = cache_fox_k.reshape(depth, bd, p, fox_w)
    cache_fv = cache_fox_v.reshape(depth, bd, p, fox_w)
    cache_dk = cache_diff_k.reshape(depth, bd, p, diff_w)
    cache_dv = cache_diff_v.reshape(depth, bd, p, diff_w)

    cs_p = jnp.tile(_rope_table(jnp.arange(n)), (b, 1))
    cs_s = jnp.tile(_rope_table(p + jnp.arange(t_new)), (bd, 1))

    meta = jnp.broadcast_to(meta_tokens[None].astype(x_prompt.dtype), (b, N_META, d))
    h = _input_layer_norm(jnp.concatenate([meta, x_prompt], axis=1).reshape(b * n, d), ln_in_g, ln_in_b)
    s = _input_layer_norm(x_sample.reshape(bd * t_new, d), ln_in_g, ln_in_b)

    outs_p = [[] for _ in range(5)]
    outs_s = [[] for _ in range(5)]
    for l in range(depth):
        lam_init = 0.8 - 0.6 * math.exp(-0.3 * l)
        lam_params = [a[l].reshape(1, HEAD_DIM).astype(F32) for a in (lambda_q1, lambda_k1, lambda_q2, lambda_k2)]
        g_sub = subln_g[l].reshape(1, 2 * HEAD_DIM).astype(F32)

        def channel_mix(x, oa, ob):
            m = _merge(x, oa, ob, w_ga[l], w_gb[l], w_a[l], w_b[l])
            x = _out_proj_ln(x, m, w_o[l], ln1_g[l], ln1_b[l], alpha)
            return _mlp_ln(x, w_u[l], w_d[l], ln2_g[l], ln2_b[l], alpha)

        qa, ka, va, lf = _project_fox(h, w_fox[l], w_f[l], b_f_pad[l])
        qb, kb, vb = _project_diff(h, w_diff[l], cs_p)
        lf = lf.reshape(b, n, FOX_HEADS)
        shape3 = lambda a: a.reshape(b, n, a.shape[-1])
        oa = _fox_prompt(shape3(qa), shape3(ka), shape3(va), _fox_bias(lf))
        ob = _diff_prompt(shape3(qb), shape3(kb), shape3(vb), lam_params, g_sub, lam_init)
        for dst, a in zip(outs_p, (ka.reshape(b, n, FOX_HEADS, HEAD_DIM), va.reshape(b, n, FOX_HEADS, HEAD_DIM), lf,
                                   kb.reshape(b, n, DIFF_HEADS, 2 * HEAD_DIM), vb.reshape(b, n, DIFF_HEADS, 2 * HEAD_DIM))):
            dst.append(a)
        h = channel_mix(h, oa.reshape(b * n, fox_w), ob.reshape(b * n, diff_w))

        qa2, ka2, va2, lf2 = _project_fox(s, w_fox[l], w_f[l], b_f_pad[l])
        qb2, kb2, vb2 = _project_diff(s, w_diff[l], cs_s)
        lf2 = lf2.reshape(bd, t_new, FOX_HEADS)
        c_all = _cumulative_logf(jnp.concatenate([cache_fox_logf[l].astype(F32), lf2], axis=1))
        new3 = lambda a: a.reshape(bd, t_new, a.shape[-1])
        oa2 = _fox_sample(new3(qa2), cache_fk, cache_fv, l, new3(ka2), new3(va2), c_all)
        ob2 = _diff_sample(new3(qb2), cache_dk, cache_dv, l, new3(kb2), new3(vb2), lam_params, g_sub, lam_init)
        for dst, a in zip(outs_s, (ka2.reshape(bd, t_new, FOX_HEADS, HEAD_DIM), va2.reshape(bd, t_new, FOX_HEADS, HEAD_DIM), lf2,
                                   kb2.reshape(bd, t_new, DIFF_HEADS, 2 * HEAD_DIM),
                                   vb2.reshape(bd, t_new, DIFF_HEADS, 2 * HEAD_DIM))):
            dst.append(a)
        s = channel_mix(s, oa2.reshape(bd * t_new, fox_w), ob2.reshape(bd * t_new, diff_w))

    y_prompt = h.reshape(b, n, d)[:, N_META:]
    y_sample = s.reshape(bd, t_new, d)
    return (y_prompt, y_sample, *(jnp.stack(o) for o in outs_p), *(jnp.stack(o) for o in outs_s))
```

```python
import functools
import math

import jax
import jax.numpy as jnp
from jax import lax
from jax.experimental import pallas as pl
from jax.experimental.pallas import tpu as pltpu

N_META = 16
CHUNK = 64
FOX_HEADS = 8
DIFF_HEADS = 4
HEAD_DIM = 128
ROT_DIM = 32
ROPE_THETA = 500000.0
LN_EPS = 1e-5
RMS_EPS = 1e-5
NEG = -1e30

LANES = 128
VMEM_LIMIT_BYTES = 56 * 2**20

ROW_TILE = 512
ATT_TILE = 512
FF_TILE = 1024
COL_CHUNK = 512
ATT_STRIP = 256
ATT_STRIPS = 8
ATT_AHEAD = 6
ONES_ROWS = 16
BIAS_PARTS = 3
LOG2E = math.log2(math.e)

F32 = jnp.float32
BF16 = jnp.bfloat16


def _params(*sem):
    return pltpu.CompilerParams(dimension_semantics=sem, vmem_limit_bytes=VMEM_LIMIT_BYTES)


def _resident(shape):
    return pl.BlockSpec(shape, lambda *_: (0,) * len(shape), pipeline_mode=pl.Buffered(1))


def _layer_norm(x, g, b):
    mu = jnp.mean(x, axis=-1, keepdims=True)
    xc = x - mu
    var = jnp.mean(xc * xc, axis=-1, keepdims=True)
    return xc * lax.rsqrt(var + LN_EPS) * g + b


def _dot(a, b):
    return jnp.dot(a, b, preferred_element_type=F32)


def _dot_nt(a, b):
    return lax.dot_general(a, b, (((1,), (1,)), ((), ())), preferred_element_type=F32)


def _ln_kernel(x_ref, g_ref, b_ref, o_ref):
    o_ref[...] = _layer_norm(x_ref[...], g_ref[...], b_ref[...])


def _input_layer_norm(x, g, b):
    t, d = x.shape
    tm = min(ROW_TILE, t)
    row = pl.BlockSpec((tm, d), lambda i: (i, 0))
    vec = pl.BlockSpec((1, d), lambda i: (0, 0))
    return pl.pallas_call(
        _ln_kernel, grid=(pl.cdiv(t, tm),), in_specs=[row, vec, vec], out_specs=row,
        out_shape=jax.ShapeDtypeStruct((t, d), F32), compiler_params=_params("parallel"),
        name="input_ln")(x, g.reshape(1, d), b.reshape(1, d))


def _store_cache(cache_ref, x):
    dim = cache_ref.shape[-1]
    for head in range(cache_ref.shape[-2]):
        cache_ref[0, :, head, :] = x[:, head * dim:(head + 1) * dim]


def _cache_call(kernel_fn, name, layer, caches, row_start, t, tm, in_specs, args, out_specs, out_shape):
    _, _, heads, dim = caches[0].shape
    cache_spec = pl.BlockSpec((pl.Element(1), pl.Element(tm), pl.Element(heads), pl.Element(dim)),
                              lambda i: (layer, row_start(i), 0, 0))
    n_in, n_out = len(in_specs), len(out_specs)
    return pl.pallas_call(
        kernel_fn, grid=(t // tm,),
        in_specs=in_specs + [pl.BlockSpec(memory_space=pl.ANY)] * 2,
        out_specs=out_specs + [cache_spec, cache_spec],
        out_shape=out_shape + [jax.ShapeDtypeStruct(c.shape, c.dtype) for c in caches],
        input_output_aliases={n_in: n_out, n_in + 1: n_out + 1},
        compiler_params=_params("parallel"), name=name)(*args, *caches)


def _proj_fox_kernel(h_ref, w_ref, wf_ref, bf_ref, *refs):
    q_ref, k_ref, v_ref, lf_ref, kc_ref, vc_ref = refs[-6:]
    hb = h_ref[...].astype(BF16)
    w = q_ref.shape[1]
    q_ref[...] = _dot(hb, w_ref[:, 0:w]).astype(BF16)
    for part, (ref, cache_ref) in enumerate(((k_ref, kc_ref), (v_ref, vc_ref)), start=1):
        z = _dot(hb, w_ref[:, part * w:(part + 1) * w])
        ref[...] = z.astype(BF16)
        _store_cache(cache_ref, z)
    fa = _dot(hb, wf_ref[...]) + bf_ref[...]
    lf = jnp.minimum(fa, 0.0) - jnp.log1p(jnp.exp(-jnp.abs(fa)))
    lf_ref[...] = lf[:, :lf_ref.shape[1]]


def _project_fox(h, w, wf, bf, layer, caches, tm, row_start):
    t, d = h.shape
    width = w.shape[1] // 3
    rows = lambda n: pl.BlockSpec((tm, n), lambda i: (i, 0))
    act = jax.ShapeDtypeStruct((t, width), BF16)
    return _cache_call(
        _proj_fox_kernel, "proj_fox", layer, caches, row_start, t, tm,
        [rows(d), _resident(w.shape), _resident(wf.shape), _resident(bf.shape)], (h, w, wf, bf),
        [rows(width), rows(width), rows(width), rows(FOX_HEADS)],
        [act, act, act, jax.ShapeDtypeStruct((t, FOX_HEADS), F32)])


def _proj_diff_kernel(h_ref, w_ref, cs_ref, *refs):
    q_ref, k_ref, v_ref, kc_ref, vc_ref = refs[-5:]
    hb = h_ref[...].astype(BF16)
    tm, w = q_ref.shape
    cos = cs_ref[:, 0:LANES]
    sin = cs_ref[:, LANES:2 * LANES]
    half = ROT_DIM // 2
    first = lax.broadcasted_iota(jnp.int32, (tm, LANES), 1) < half

    def rope(x):
        partner = jnp.where(first, pltpu.roll(x, LANES - half, 1), pltpu.roll(x, half, 1))
        return x * cos + partner * sin

    def rotated(z):
        return jnp.concatenate([rope(z[:, g * LANES:(g + 1) * LANES]) for g in range(w // LANES)], axis=1)

    q_ref[...] = rotated(_dot(hb, w_ref[:, 0:w])).astype(BF16)
    k = rotated(_dot(hb, w_ref[:, w:2 * w]))
    k_ref[...] = k.astype(BF16)
    _store_cache(kc_ref, k)
    v = _dot(hb, w_ref[:, 2 * w:3 * w])
    v_ref[...] = v.astype(BF16)
    _store_cache(vc_ref, v)


def _project_diff(h, w, cs, layer, caches, tm, row_start):
    t, d = h.shape
    width = w.shape[1] // 3
    rows = lambda n: pl.BlockSpec((tm, n), lambda i: (i, 0))
    act = jax.ShapeDtypeStruct((t, width), BF16)
    return _cache_call(
        _proj_diff_kernel, "proj_diff", layer, caches, row_start, t, tm,
        [rows(d), _resident(w.shape), rows(2 * LANES)], (h, w, cs),
        [rows(width), rows(width), rows(width)], [act, act, act])


def _rope_table(pos):
    half = ROT_DIM // 2
    inv_freq = ROPE_THETA ** (-jnp.arange(half, dtype=F32) / half)
    ang = pos.astype(F32)[:, None] * inv_freq[None, :]
    cos, sin = jnp.cos(ang), jnp.sin(ang)
    n = pos.shape[0]
    ones = jnp.ones((n, LANES - ROT_DIM), F32)
    zeros = jnp.zeros((n, LANES - ROT_DIM), F32)
    return jnp.concatenate([cos, cos, ones, -sin, sin, zeros], axis=1)


def _split_bf16(x, parts):
    out, rest = [], x
    for _ in range(parts):
        t = rest.astype(BF16)
        out.append(t)
        rest = rest - t.astype(F32)
    return out


def _fox_bias_kernel(lf0_ref, lf1_ref, a0_ref, a1_ref, *, inv_scale):
    row = lax.broadcasted_iota(jnp.int32, (LANES, LANES), 0)
    col = lax.broadcasted_iota(jnp.int32, (LANES, LANES), 1)
    tri = jnp.where(col <= row, 1.0, 0.0).astype(BF16)
    spread = [jnp.where(col == BIAS_PARTS * row + t, 1.0, 0.0).astype(BF16) for t in range(BIAS_PARTS)]
    carry = jnp.zeros((1, LANES), F32)
    for lf_ref, a_ref in ((lf0_ref, a0_ref), (lf1_ref, a1_ref)):
        n = lf_ref.shape[0]
        for r0 in range(0, n, LANES):
            rows = min(LANES, n - r0)
            x = lf_ref[r0:r0 + rows]
            cs = carry
            for term in _split_bf16(x, BIAS_PARTS):
                cs = cs + _dot(tri[:rows, :rows], term)
            carry = cs[rows - 1:rows]
            a = cs * (-inv_scale)
            out = jnp.zeros((rows, LANES), F32)
            for term, e in zip(_split_bf16(a, BIAS_PARTS), spread):
                out = out + _dot(term, e)
            a_ref[r0:r0 + rows] = out.astype(BF16)


def _fox_bias(logf0, logf1):
    b, _, h = logf0.shape
    pad = lambda x: jnp.pad(x, ((0, 0), (0, 0), (0, LANES - h)))
    spec = lambda x: pl.BlockSpec((None, x.shape[1], LANES), lambda i: (i, 0, 0))
    return pl.pallas_call(
        functools.partial(_fox_bias_kernel, inv_scale=HEAD_DIM ** 0.5), grid=(b,),
        in_specs=[spec(logf0), spec(logf1)], out_specs=[spec(logf0), spec(logf1)],
        out_shape=[jax.ShapeDtypeStruct((b, x.shape[1], LANES), BF16) for x in (logf0, logf1)],
        compiler_params=_params("parallel"), name="fox_bias")(pad(logf0), pad(logf1))


def _attend(stages, c_exp):
    pending = {}
    for t in range(min(ATT_AHEAD, len(stages))):
        pending[t] = stages[t][0]()
    for t, (_, mask, (m_ref, acc_ref), slot, w, vt_blk) in enumerate(stages):
        if t + ATT_AHEAD < len(stages):
            pending[t + ATT_AHEAD] = stages[t + ATT_AHEAD][0]()
        st = pending.pop(t)
        if mask is not None:
            st = jnp.where(mask, st, NEG)
        m_prev = m_ref[slot, :, 0:w]
        m_new = jnp.maximum(m_prev, jnp.max(st, axis=0, keepdims=True))
        alpha = jnp.exp2((m_prev - m_new) * c_exp)
        p = jnp.exp2((st - m_new) * c_exp)
        m_ref[slot, :, 0:w] = m_new
        vt = vt_blk() if callable(vt_blk) else vt_blk
        acc_ref[slot, :, 0:w] = alpha * acc_ref[slot, :, 0:w] + _dot(vt, p.astype(BF16))


def _reset(state, slot, w):
    m_ref, acc_ref = state
    m_ref[slot, :, 0:w] = jnp.full((1, w), NEG, F32)
    acc_ref[slot, :, 0:w] = jnp.zeros((acc_ref.shape[1], w), F32)


def _normalised(state, slot, w):
    m_ref, acc_ref = state
    dv = acc_ref.shape[1] - ONES_ROWS
    return acc_ref[slot, 0:dv, 0:w] * (1.0 / acc_ref[slot, dv:dv + 1, 0:w])


def _state_scratch(slots, dv):
    return [pltpu.VMEM((slots, 1, ATT_STRIP), F32), pltpu.VMEM((slots, dv + ONES_ROWS, ATT_STRIP), F32)]


def _pad_rows(x, rows):
    return jnp.concatenate([x, jnp.zeros((rows - x.shape[0], x.shape[1]), x.dtype)], axis=0)


def _transpose_values(v_ref, vm_ref, vt, vtm):
    s_len, dv = v_ref.shape
    sw = ATT_STRIP
    vt[:, dv:, :] = jnp.ones((vt.shape[0], ONES_ROWS, sw), BF16)
    vtm[dv:, :] = jnp.ones((ONES_ROWS, LANES), BF16)
    for j in range(s_len // sw):
        for c in range(sw // LANES):
            r0 = j * sw + c * LANES
            for e in range(dv // LANES):
                vt[j, e * LANES:(e + 1) * LANES, c * LANES:(c + 1) * LANES] = (
                    v_ref[r0:r0 + LANES, e * LANES:(e + 1) * LANES].T)
    vmeta = _pad_rows(vm_ref[...], LANES)
    for e in range(dv // LANES):
        vtm[e * LANES:(e + 1) * LANES] = vmeta[:, e * LANES:(e + 1) * LANES].T


def _fox_prompt_kernel(q_ref, k_ref, v_ref, a_ref, qm_ref, km_ref, vm_ref, am_ref, o_ref, om_ref,
                       qa, ka, vt, vtm, m_ref, acc_ref, *, ns):
    s_len = q_ref.shape[0]
    sw = ATT_STRIP
    tq = ns * sw
    d = HEAD_DIM
    c_exp = d ** -0.5 * LOG2E
    state = (m_ref, acc_ref)
    lane = lax.broadcasted_iota(jnp.int32, (1, LANES), 1)
    first = pl.program_id(1) * BIAS_PARTS
    onehot = jnp.where((lane >= first) & (lane < first + BIAS_PARTS), 1.0, 0.0).astype(BF16)
    qa[0:s_len, 0:d] = q_ref[...]
    qa[s_len:, 0:d] = _pad_rows(qm_ref[...], LANES)
    qa[:, d:] = jnp.broadcast_to(onehot, (s_len + LANES, LANES))
    ka[0:s_len, 0:d] = k_ref[...]
    ka[0:s_len, d:] = a_ref[...]
    ka[s_len:, 0:d] = _pad_rows(km_ref[...], LANES)
    ka[s_len:, d:] = _pad_rows(am_ref[...], LANES)
    _transpose_values(v_ref, vm_ref, vt, vtm)
    k_meta = ka[s_len:s_len + LANES]

    kidx = lax.broadcasted_iota(jnp.int32, (LANES, LANES), 0)
    qidx = lax.broadcasted_iota(jnp.int32, (LANES, LANES), 1)
    _reset(state, 0, LANES)
    _attend([(lambda: _dot_nt(k_meta, qa[s_len:s_len + LANES]), (kidx <= qidx) & (kidx < N_META), state, 0, LANES,
              vtm[...])], c_exp)
    om_ref[...] = _normalised(state, 0, LANES).T[0:N_META].astype(om_ref.dtype)

    meta_valid = lax.broadcasted_iota(jnp.int32, (LANES, sw), 0) < N_META
    diag = lax.broadcasted_iota(jnp.int32, (sw, sw), 0) <= lax.broadcasted_iota(jnp.int32, (sw, sw), 1)

    def q_block(i, carry):
        r0 = pl.multiple_of(i * tq, tq)

        def scores(k_blk, c):
            return lambda: _dot_nt(k_blk, qa[pl.ds(r0 + c * sw, sw)])

        for c in range(ns):
            _reset(state, c, sw)
        _attend([(scores(k_meta, c), meta_valid, state, c, sw, vtm[...]) for c in range(ns)], c_exp)

        def kv_block(j, cc):
            k_blk = ka[pl.ds(pl.multiple_of(j * sw, sw), sw)]
            _attend([(scores(k_blk, c), None, state, c, sw, vt[j]) for c in range(ns)], c_exp)
            return cc

        lax.fori_loop(0, i * ns, kv_block, 0)
        stages = []
        for dd in range(ns):
            k_blk = ka[pl.ds(r0 + dd * sw, sw)]
            stages += [(scores(k_blk, c), diag if c == dd else None, state, c, sw, vt[i * ns + dd])
                       for c in range(dd, ns)]
        _attend(stages, c_exp)
        for c in range(ns):
            o_ref[pl.ds(r0 + c * sw, sw)] = _normalised(state, c, sw).T.astype(o_ref.dtype)
        return carry

    lax.fori_loop(0, s_len // tq, q_block, 0)


def _fox_prompt(frames, meta):
    b, s_len, width = frames[0].shape
    d = HEAD_DIM
    ns = min(ATT_STRIPS, s_len // ATT_STRIP)
    head = lambda rows: pl.BlockSpec((None, rows, d), lambda i, h: (i, 0, h))
    bias = lambda rows: pl.BlockSpec((None, rows, LANES), lambda i, h: (i, 0, 0))
    group = lambda rows: [head(rows)] * 3 + [bias(rows)]
    return pl.pallas_call(
        functools.partial(_fox_prompt_kernel, ns=ns), grid=(b, width // d),
        in_specs=group(s_len) + group(N_META), out_specs=[head(s_len), head(N_META)],
        out_shape=[jax.ShapeDtypeStruct(frames[0].shape, BF16), jax.ShapeDtypeStruct(meta[0].shape, BF16)],
        scratch_shapes=[pltpu.VMEM((s_len + LANES, 2 * d), BF16), pltpu.VMEM((s_len + LANES, 2 * d), BF16),
                        pltpu.VMEM((s_len // ATT_STRIP, d + ONES_ROWS, ATT_STRIP), BF16),
                        pltpu.VMEM((d + ONES_ROWS, LANES), BF16)]
        + _state_scratch(ns, d),
        compiler_params=_params("parallel", "parallel"), name="fox_prompt")(*frames, *meta)


def _values_t(v):
    keys, dv = v.shape
    cols = [jnp.concatenate([v[c * LANES:(c + 1) * LANES, e * LANES:(e + 1) * LANES].T
                             for e in range(dv // LANES)], axis=0) for c in range(keys // LANES)]
    return jnp.concatenate([jnp.concatenate(cols, axis=1).astype(BF16), jnp.ones((ONES_ROWS, keys), BF16)], axis=0)


def _fox_sample_kernel(q_ref, kc_ref, vc_ref, ac_ref, kn_ref, vn_ref, an_ref, o_ref, qa, m_ref, acc_ref):
    j = pl.program_id(1)
    t = q_ref.shape[0]
    d = HEAD_DIM
    heads = kc_ref.shape[1]
    c_exp = d ** -0.5 * LOG2E
    state = (m_ref, acc_ref)

    @pl.when(j == 0)
    def _():
        lane = lax.broadcasted_iota(jnp.int32, (1, LANES), 1)
        for h in range(heads):
            onehot = jnp.where((lane >= h * BIAS_PARTS) & (lane < (h + 1) * BIAS_PARTS), 1.0, 0.0).astype(BF16)
            qa[h, :, 0:d] = _pad_rows(q_ref[:, h * d:(h + 1) * d], LANES)
            qa[h, :, d:] = jnp.broadcast_to(onehot, (LANES, LANES))
            _reset(state, h, LANES)

    a_blk = ac_ref[...]
    _attend([(functools.partial(
        lambda h: _dot_nt(jnp.concatenate([kc_ref[:, h, :].astype(BF16), a_blk], axis=1), qa[h]), h),
              None, state, h, LANES, functools.partial(lambda h: _values_t(vc_ref[:, h, :]), h))
             for h in range(heads)], c_exp)

    @pl.when(j == pl.num_programs(1) - 1)
    def _():
        causal = (lax.broadcasted_iota(jnp.int32, (LANES, LANES), 0)
                  <= lax.broadcasted_iota(jnp.int32, (LANES, LANES), 1))
        a_new = an_ref[...]
        _attend([(functools.partial(
            lambda h: _dot_nt(_pad_rows(jnp.concatenate([kn_ref[:, h * d:(h + 1) * d], a_new], axis=1), LANES),
                              qa[h]), h),
                  causal, state, h, LANES,
                  functools.partial(lambda h: _values_t(_pad_rows(vn_ref[:, h * d:(h + 1) * d], LANES)), h))
                 for h in range(heads)], c_exp)
        for h in range(heads):
            o_ref[:, h * d:(h + 1) * d] = _normalised(state, h, LANES).T[0:t].astype(o_ref.dtype)


def _fox_sample(q, cache_k, cache_v, layer, a_cache, k_new, v_new, a_new):
    b, t, width = q.shape
    d = HEAD_DIM
    heads = width // d
    p = cache_k.shape[2]
    tk = min(ATT_TILE, p)
    assert t <= LANES and p % tk == 0
    new = lambda n: pl.BlockSpec((None, t, n), lambda i, j: (i, 0, 0))
    cache = pl.BlockSpec((None, None, tk, heads, d), lambda i, j: (layer, i, j, 0, 0))
    return pl.pallas_call(
        _fox_sample_kernel, grid=(b, p // tk),
        in_specs=[new(width), cache, cache, pl.BlockSpec((None, tk, LANES), lambda i, j: (i, j, 0)),
                  new(width), new(width), new(LANES)],
        out_specs=new(width), out_shape=jax.ShapeDtypeStruct(q.shape, BF16),
        scratch_shapes=[pltpu.VMEM((heads, LANES, 2 * d), BF16)] + _state_scratch(heads, d),
        compiler_params=_params("parallel", "arbitrary"), name="fox_sample")(
            q, cache_k, cache_v, a_cache, k_new, v_new, a_new)


def _diff_lambda(lq1_ref, lk1_ref, lq2_ref, lk2_ref, lam_init):
    d1 = jnp.sum(lq1_ref[...] * lk1_ref[...], axis=1, keepdims=True)
    d2 = jnp.sum(lq2_ref[...] * lk2_ref[...], axis=1, keepdims=True)
    return jnp.exp(d1) - jnp.exp(d2) + lam_init


def _diff_finish_t(s1, s2, slot, w, lam, g_ref, lam_init):
    ot = _normalised(s1, slot, w) - lam * _normalised(s2, slot, w)
    ot = ot * lax.rsqrt(jnp.mean(ot * ot, axis=0, keepdims=True) + RMS_EPS)
    return ot.T * g_ref[...] * (1.0 - lam_init)


def _diff_prompt_kernel(q_ref, k_ref, v_ref, qm_ref, km_ref, vm_ref, lq1_ref, lk1_ref, lq2_ref, lk2_ref, g_ref,
                        o_ref, om_ref, vt, vtm, m1, a1, m2, a2, *, ns, lam_init):
    s_len = q_ref.shape[0]
    sw = ATT_STRIP
    tq = ns * sw
    d = HEAD_DIM
    c_exp = d ** -0.5 * LOG2E
    states = ((m1, a1), (m2, a2))
    lam = _diff_lambda(lq1_ref, lk1_ref, lq2_ref, lk2_ref, lam_init)
    _transpose_values(v_ref, vm_ref, vt, vtm)

    def both(keys, queries, c, w, mask, vt_blk):
        return [(functools.partial(lambda lo: _dot_nt(keys(lo), queries(lo)), e * d), mask, states[e], c, w, vt_blk)
                for e in range(2)]

    def meta_keys(lo):
        return _pad_rows(km_ref[:, lo:lo + d], LANES)

    def frame_rows(ref, rows):
        return lambda lo: ref[rows, lo:lo + d]

    for st in states:
        _reset(st, 0, LANES)
    real_keys = lax.broadcasted_iota(jnp.int32, (LANES, LANES), 0) < N_META
    _attend(both(meta_keys, lambda lo: _pad_rows(qm_ref[:, lo:lo + d], LANES), 0, LANES, real_keys, vtm[...]), c_exp)
    om_ref[...] = _diff_finish_t(states[0], states[1], 0, LANES, lam, g_ref, lam_init)[0:N_META].astype(om_ref.dtype)

    meta_valid = lax.broadcasted_iota(jnp.int32, (LANES, sw), 0) < N_META
    diag = (lax.broadcasted_iota(jnp.int32, (sw, sw), 0) // CHUNK
            <= lax.broadcasted_iota(jnp.int32, (sw, sw), 1) // CHUNK)

    def q_block(i, carry):
        r0 = pl.multiple_of(i * tq, tq)

        def q_rows(c):
            return pl.ds(r0 + c * sw, sw)

        for c in range(ns):
            for st in states:
                _reset(st, c, sw)
        stages = []
        for c in range(ns):
            stages += both(meta_keys, frame_rows(q_ref, q_rows(c)), c, sw, meta_valid, vtm[...])
        _attend(stages, c_exp)

        def kv_block(j, cc):
            keys = frame_rows(k_ref, pl.ds(pl.multiple_of(j * sw, sw), sw))
            stages = []
            for c in range(ns):
                stages += both(keys, frame_rows(q_ref, q_rows(c)), c, sw, None, vt[j])
            _attend(stages, c_exp)
            return cc

        lax.fori_loop(0, i * ns, kv_block, 0)
        stages = []
        for dd in range(ns):
            keys = frame_rows(k_ref, pl.ds(r0 + dd * sw, sw))
            for c in range(dd, ns):
                stages += both(keys, frame_rows(q_ref, q_rows(c)), c, sw, diag if c == dd else None, vt[i * ns + dd])
        _attend(stages, c_exp)
        for c in range(ns):
            o_ref[q_rows(c)] = _diff_finish_t(states[0], states[1], c, sw, lam, g_ref, lam_init).astype(o_ref.dtype)
        return carry

    lax.fori_loop(0, s_len // tq, q_block, 0)


def _lambda_specs():
    return [pl.BlockSpec((1, HEAD_DIM), lambda i, h: (0, 0))] * 4 + [
        pl.BlockSpec((1, 2 * HEAD_DIM), lambda i, h: (0, 0))]


def _diff_prompt(frames, meta, lam_params, g, lam_init):
    b, s_len, width = frames[0].shape
    dv = 2 * HEAD_DIM
    ns = min(ATT_STRIPS, s_len // ATT_STRIP)
    head = lambda rows: pl.BlockSpec((None, rows, dv), lambda i, h: (i, 0, h))
    return pl.pallas_call(
        functools.partial(_diff_prompt_kernel, ns=ns, lam_init=lam_init), grid=(b, width // dv),
        in_specs=[head(s_len)] * 3 + [head(N_META)] * 3 + _lambda_specs(),
        out_specs=[head(s_len), head(N_META)],
        out_shape=[jax.ShapeDtypeStruct(frames[0].shape, BF16), jax.ShapeDtypeStruct(meta[0].shape, BF16)],
        scratch_shapes=[pltpu.VMEM((s_len // ATT_STRIP, dv + ONES_ROWS, ATT_STRIP), BF16),
                        pltpu.VMEM((dv + ONES_ROWS, LANES), BF16)]
        + _state_scratch(ns, dv) + _state_scratch(ns, dv),
        compiler_params=_params("parallel", "parallel"), name="diff_prompt")(*frames, *meta, *lam_params, g)


def _diff_sample_kernel(q_ref, kc_ref, vc_ref, kn_ref, vn_ref, lq1_ref, lk1_ref, lq2_ref, lk2_ref, g_ref, o_ref,
                        qp, m1, a1, m2, a2, *, lam_init):
    j = pl.program_id(1)
    t = q_ref.shape[0]
    d = HEAD_DIM
    heads = kc_ref.shape[1]
    c_exp = d ** -0.5 * LOG2E
    states = ((m1, a1), (m2, a2))

    @pl.when(j == 0)
    def _():
        for h in range(heads):
            for e in range(2):
                qp[2 * h + e] = _pad_rows(q_ref[:, (2 * h + e) * d:(2 * h + e + 1) * d], LANES)
                _reset(states[e], h, LANES)

    def stages(keys, values, mask):
        return [(functools.partial(lambda h, e: _dot_nt(keys(h, e), qp[2 * h + e]), h, e), mask, states[e], h, LANES,
                 functools.partial(lambda h: _values_t(values(h)), h)) for h in range(heads) for e in range(2)]

    _attend(stages(lambda h, e: kc_ref[:, h, e * d:(e + 1) * d].astype(BF16), lambda h: vc_ref[:, h, :], None), c_exp)

    @pl.when(j == pl.num_programs(1) - 1)
    def _():
        lam = _diff_lambda(lq1_ref, lk1_ref, lq2_ref, lk2_ref, lam_init)
        real = lax.broadcasted_iota(jnp.int32, (LANES, LANES), 0) < t
        _attend(stages(lambda h, e: _pad_rows(kn_ref[:, (2 * h + e) * d:(2 * h + e + 1) * d], LANES),
                       lambda h: _pad_rows(vn_ref[:, 2 * h * d:2 * (h + 1) * d], LANES), real), c_exp)
        for h in range(heads):
            o_ref[:, 2 * h * d:2 * (h + 1) * d] = _diff_finish_t(
                states[0], states[1], h, LANES, lam, g_ref, lam_init)[0:t].astype(o_ref.dtype)


def _diff_sample(q, cache_k, cache_v, layer, k_new, v_new, lam_params, g, lam_init):
    b, t, width = q.shape
    dv = 2 * HEAD_DIM
    heads = width // dv
    p = cache_k.shape[2]
    tk = min(ATT_TILE, p)
    assert t <= LANES and p % tk == 0
    new = pl.BlockSpec((None, t, width), lambda i, j: (i, 0, 0))
    cache = pl.BlockSpec((None, None, tk, heads, dv), lambda i, j: (layer, i, j, 0, 0))
    return pl.pallas_call(
        functools.partial(_diff_sample_kernel, lam_init=lam_init), grid=(b, p // tk),
        in_specs=[new, cache, cache, new, new] + _lambda_specs(),
        out_specs=new, out_shape=jax.ShapeDtypeStruct(q.shape, BF16),
        scratch_shapes=[pltpu.VMEM((2 * heads, LANES, HEAD_DIM), BF16)]
        + _state_scratch(heads, dv) + _state_scratch(heads, dv),
        compiler_params=_params("parallel", "arbitrary"), name="diff_sample")(
            q, cache_k, cache_v, k_new, v_new, *lam_params, g)


def _merge_kernel(h_ref, oa_ref, ob_ref, wga_ref, wgb_ref, wa_ref, wb_ref, m_ref):
    hb = h_ref[...].astype(BF16)
    oa, ob = oa_ref[...], ob_ref[...]
    for c in range(m_ref.shape[1] // COL_CHUNK):
        sl = slice(c * COL_CHUNK, (c + 1) * COL_CHUNK)
        ga = jax.nn.sigmoid(_dot(hb, wga_ref[:, sl]))
        gb = jax.nn.sigmoid(_dot(hb, wgb_ref[:, sl]))
        m_ref[:, sl] = (ga * _dot(oa, wa_ref[:, sl]) + gb * _dot(ob, wb_ref[:, sl])).astype(m_ref.dtype)


def _merge(h, oa, ob, wga, wgb, wa, wb):
    t, d = h.shape
    tm = min(ROW_TILE, t)
    rows = lambda n: pl.BlockSpec((tm, n), lambda i: (i, 0))
    return pl.pallas_call(
        _merge_kernel, grid=(pl.cdiv(t, tm),),
        in_specs=[rows(d), rows(oa.shape[1]), rows(ob.shape[1]),
                  _resident(wga.shape), _resident(wgb.shape), _resident(wa.shape), _resident(wb.shape)],
        out_specs=rows(d), out_shape=jax.ShapeDtypeStruct((t, d), BF16),
        compiler_params=_params("parallel"), name="gated_merge")(h, oa, ob, wga, wgb, wa, wb)


def _out_ln_kernel(h_ref, m_ref, w_ref, g_ref, b_ref, o_ref, *, alpha):
    y = _dot(m_ref[...], w_ref[...])
    o_ref[...] = _layer_norm(alpha * h_ref[...] + y, g_ref[...], b_ref[...])


def _out_proj_ln(h, m, w, g, b, alpha):
    t, d = h.shape
    tm = min(ROW_TILE, t)
    row = pl.BlockSpec((tm, d), lambda i: (i, 0))
    vec = pl.BlockSpec((1, d), lambda i: (0, 0))
    return pl.pallas_call(
        functools.partial(_out_ln_kernel, alpha=alpha), grid=(pl.cdiv(t, tm),),
        in_specs=[row, row, _resident(w.shape), vec, vec], out_specs=row,
        out_shape=jax.ShapeDtypeStruct((t, d), F32),
        compiler_params=_params("parallel"), name="out_proj_ln")(h, m, w, g.reshape(1, d), b.reshape(1, d))


def _mlp_kernel(x_ref, wu_ref, wd_ref, g_ref, b_ref, o_ref, xb_ref, acc_ref, *, alpha):
    f = pl.program_id(1)

    @pl.when(f == 0)
    def _():
        xb_ref[...] = x_ref[...].astype(BF16)
        acc_ref[...] = jnp.zeros(acc_ref.shape, F32)

    u = jnp.maximum(_dot(xb_ref[...], wu_ref[...]), 0.0)
    acc_ref[...] += _dot((u * u).astype(BF16), wd_ref[...])

    @pl.when(f == pl.num_programs(1) - 1)
    def _():
        o_ref[...] = _layer_norm(alpha * x_ref[...] + acc_ref[...], g_ref[...], b_ref[...])


def _mlp_ln(x, wu, wd, g, b, alpha):
    t, d = x.shape
    dff = wu.shape[1]
    tm = min(ROW_TILE, t)
    tf = min(FF_TILE, dff)
    row = pl.BlockSpec((tm, d), lambda i, f: (i, 0))
    vec = pl.BlockSpec((1, d), lambda i, f: (0, 0))
    return pl.pallas_call(
        functools.partial(_mlp_kernel, alpha=alpha), grid=(pl.cdiv(t, tm), dff // tf),
        in_specs=[row, pl.BlockSpec((d, tf), lambda i, f: (0, f)), pl.BlockSpec((tf, d), lambda i, f: (f, 0)),
                  vec, vec],
        out_specs=row, out_shape=jax.ShapeDtypeStruct((t, d), F32),
        scratch_shapes=[pltpu.VMEM((tm, d), BF16), pltpu.VMEM((tm, d), F32)],
        compiler_params=_params("parallel", "arbitrary"), name="mlp_ln")(
            x, wu, wd, g.reshape(1, d), b.reshape(1, d))


def kernel(x_prompt, x_sample, cache_fox_k, cache_fox_v, cache_fox_logf, cache_diff_k, cache_diff_v, meta_tokens, ln_in_g, ln_in_b, w_in, b_f, lambda_q1, lambda_k1, lambda_q2, lambda_k2, subln_g, w_br_a, w_br_b, w_out, ln1_g, ln1_b, w_up, w_down, ln2_g, ln2_b):
    b, s_len, d = x_prompt.shape
    bd, t_new, _ = x_sample.shape
    depth = w_in.shape[0]
    p = cache_fox_k.shape[2]
    n = N_META + s_len
    alpha = (2 * depth) ** 0.25
    fox_w = FOX_HEADS * HEAD_DIM
    diff_w = DIFF_HEADS * 2 * HEAD_DIM

    o_f = 3 * fox_w
    o_b = o_f + FOX_HEADS
    o_g = o_b + 3 * diff_w
    w_in_b = w_in.astype(BF16)
    w_fox = w_in_b[:, :, :o_f]
    w_f = jnp.pad(w_in_b[:, :, o_f:o_b], ((0, 0), (0, 0), (0, LANES - FOX_HEADS)))
    w_diff = w_in_b[:, :, o_b:o_g]
    w_ga = w_in_b[:, :, o_g:o_g + d]
    w_gb = w_in_b[:, :, o_g + d:]
    b_f_pad = jnp.pad(b_f.astype(F32), ((0, 0), (0, LANES - FOX_HEADS)))[:, None, :]
    w_a, w_b, w_o = w_br_a.astype(BF16), w_br_b.astype(BF16), w_out.astype(BF16)
    w_u, w_d = w_up.astype(BF16), w_down.astype(BF16)

    cs_f = jnp.tile(_rope_table(N_META + jnp.arange(s_len)), (b, 1))
    cs_m = jnp.tile(_rope_table(jnp.arange(N_META)), (b, 1))
    cs_s = jnp.tile(_rope_table(p + jnp.arange(t_new)), (bd, 1))

    tm_f = min(ROW_TILE, s_len)
    tm_s = min(ROW_TILE, bd * t_new)
    assert s_len % tm_f == 0 and (bd * t_new) % tm_s == 0
    blocks_per_seq = s_len // tm_f
    frame_rows = lambda i: (i // blocks_per_seq) * n + N_META + (i % blocks_per_seq) * tm_f
    meta_rows = lambda i: i * n
    sample_rows = lambda i: i * tm_s

    hf = _input_layer_norm(x_prompt.reshape(b * s_len, d), ln_in_g, ln_in_b)
    hm = jnp.tile(_input_layer_norm(meta_tokens.astype(F32), ln_in_g, ln_in_b), (b, 1))
    hs = _input_layer_norm(x_sample.reshape(bd * t_new, d), ln_in_g, ln_in_b)

    def empty_caches(rows, heads, dim):
        return [jnp.zeros((depth, rows, heads, dim), F32) for _ in range(2)]

    fox_p, diff_p = empty_caches(b * n, FOX_HEADS, HEAD_DIM), empty_caches(b * n, DIFF_HEADS, 2 * HEAD_DIM)
    fox_s, diff_s = empty_caches(bd * t_new, FOX_HEADS, HEAD_DIM), empty_caches(bd * t_new, DIFF_HEADS, 2 * HEAD_DIM)
    logf_p, logf_s = [], []
    for l in range(depth):
        lam_init = 0.8 - 0.6 * math.exp(-0.3 * l)
        lam_params = [a[l].reshape(1, HEAD_DIM).astype(F32) for a in (lambda_q1, lambda_k1, lambda_q2, lambda_k2)]
        g_sub = subln_g[l].reshape(1, 2 * HEAD_DIM).astype(F32)

        def project(x, cs, fox_c, diff_c, tm, row_start):
            qa, ka, va, lf, *fox_c = _project_fox(x, w_fox[l], w_f[l], b_f_pad[l], l, fox_c, tm, row_start)
            qb, kb, vb, *diff_c = _project_diff(x, w_diff[l], cs, l, diff_c, tm, row_start)
            return (qa, ka, va), lf, (qb, kb, vb), fox_c, diff_c

        def channel_mix(x, oa, ob):
            m = _merge(x, oa.reshape(-1, fox_w), ob.reshape(-1, diff_w), w_ga[l], w_gb[l], w_a[l], w_b[l])
            x = _out_proj_ln(x, m, w_o[l], ln1_g[l], ln1_b[l], alpha)
            return _mlp_ln(x, w_u[l], w_d[l], ln2_g[l], ln2_b[l], alpha)

        def per_seq(xs, batch):
            return tuple(x.reshape(batch, x.shape[0] // batch, x.shape[-1]) for x in xs)

        qkv_a, lf_f, qkv_b, fox_p, diff_p = project(hf, cs_f, fox_p, diff_p, tm_f, frame_rows)
        qkv_am, lf_m, qkv_bm, fox_p, diff_p = project(hm, cs_m, fox_p, diff_p, N_META, meta_rows)
        lf_f, lf_m = lf_f.reshape(b, s_len, FOX_HEADS), lf_m.reshape(b, N_META, FOX_HEADS)
        logf_p.append(jnp.concatenate([lf_m, lf_f], axis=1))
        a_m, a_f = _fox_bias(lf_m, lf_f)
        oa_f, oa_m = _fox_prompt(per_seq(qkv_a, b) + (a_f,), per_seq(qkv_am, b) + (a_m,))
        ob_f, ob_m = _diff_prompt(per_seq(qkv_b, b), per_seq(qkv_bm, b), lam_params, g_sub, lam_init)
        hf = channel_mix(hf, oa_f, ob_f)
        hm = channel_mix(hm, oa_m, ob_m)

        qkv_a, lf_s, qkv_b, fox_s, diff_s = project(hs, cs_s, fox_s, diff_s, tm_s, sample_rows)
        lf_s = lf_s.reshape(bd, t_new, FOX_HEADS)
        logf_s.append(lf_s)
        a_c, a_n = _fox_bias(cache_fox_logf[l].astype(F32), lf_s)
        qa, ka, va = per_seq(qkv_a, bd)
        qb, kb, vb = per_seq(qkv_b, bd)
        oa_s = _fox_sample(qa, cache_fox_k, cache_fox_v, l, a_c, ka, va, a_n)
        ob_s = _diff_sample(qb, cache_diff_k, cache_diff_v, l, kb, vb, lam_params, g_sub, lam_init)
        hs = channel_mix(hs, oa_s, ob_s)

    def per_layer_seq(caches, rows):
        return tuple(c.reshape(depth, c.shape[1] // rows, rows, *c.shape[2:]) for c in caches)

    fk_p, fv_p = per_layer_seq(fox_p, n)
    dk_p, dv_p = per_layer_seq(diff_p, n)
    fk_s, fv_s = per_layer_seq(fox_s, t_new)
    dk_s, dv_s = per_layer_seq(diff_s, t_new)
    return (hf.reshape(b, s_len, d), hs.reshape(bd, t_new, d), fk_p, fv_p, jnp.stack(logf_p), dk_p, dv_p,
            fk_s, fv_s, jnp.stack(logf_s), dk_s, dv_s)
```

```python
import functools
import math

import jax
import jax.numpy as jnp
from jax import lax
from jax.experimental import pallas as pl
from jax.experimental.pallas import tpu as pltpu

N_META = 16
CHUNK = 64
FOX_HEADS = 8
DIFF_HEADS = 4
HEAD_DIM = 128
ROT_DIM = 32
ROPE_THETA = 500000.0
LN_EPS = 1e-5
RMS_EPS = 1e-5
NEG = -1e30

LANES = 128
VMEM_LIMIT_BYTES = 56 * 2**20

ROW_TILE = 512
ATT_TILE = 512
FF_TILE = 1024
COL_CHUNK = 512
ATT_STRIP = 256
FOX_STRIPS = 16
DIFF_STRIPS = 8
ATT_AHEAD = 6
FOX_AHEAD = 8
ONES_ROWS = 16
BIAS_PARTS = 3
LOG2E = math.log2(math.e)

F32 = jnp.float32
BF16 = jnp.bfloat16


def _params(*sem):
    return pltpu.CompilerParams(dimension_semantics=sem, vmem_limit_bytes=VMEM_LIMIT_BYTES)


def _resident(shape):
    return pl.BlockSpec(shape, lambda *_: (0,) * len(shape), pipeline_mode=pl.Buffered(1))


def _layer_norm(x, g, b):
    mu = jnp.mean(x, axis=-1, keepdims=True)
    xc = x - mu
    var = jnp.mean(xc * xc, axis=-1, keepdims=True)
    return xc * lax.rsqrt(var + LN_EPS) * g + b


def _dot(a, b):
    return jnp.dot(a, b, preferred_element_type=F32)


def _dot_nt(a, b):
    return lax.dot_general(a, b, (((1,), (1,)), ((), ())), preferred_element_type=F32)


def _ln_kernel(x_ref, g_ref, b_ref, o_ref):
    o_ref[...] = _layer_norm(x_ref[...], g_ref[...], b_ref[...])


def _input_layer_norm(x, g, b):
    t, d = x.shape
    tm = min(ROW_TILE, t)
    row = pl.BlockSpec((tm, d), lambda i: (i, 0))
    vec = pl.BlockSpec((1, d), lambda i: (0, 0))
    return pl.pallas_call(
        _ln_kernel, grid=(pl.cdiv(t, tm),), in_specs=[row, vec, vec], out_specs=row,
        out_shape=jax.ShapeDtypeStruct((t, d), F32), compiler_params=_params("parallel"),
        name="input_ln")(x, g.reshape(1, d), b.reshape(1, d))


def _store_cache(cache_ref, x):
    rows, dim = x.shape[0], cache_ref.shape[-1]
    heads = x.shape[1] // dim
    for head in range(heads):
        part = x[:, head * dim:(head + 1) * dim]
        if len(cache_ref.shape) == 3:
            cache_ref[0, pl.ds(head, rows, stride=heads), :] = part
        else:
            cache_ref[0, :, head, :] = part


def _cache_call(kernel_fn, name, layer, caches, heads, row_start, t, tm, in_specs, args, out_specs, out_shape):
    dim = caches[0].shape[-1]
    if caches[0].ndim == 3:
        cache_spec = pl.BlockSpec((pl.Element(1), pl.Element(tm * heads), pl.Element(dim)),
                                  lambda i: (layer, pl.multiple_of(row_start(i) * heads, heads), 0))
    else:
        cache_spec = pl.BlockSpec((pl.Element(1), pl.Element(tm), pl.Element(heads), pl.Element(dim)),
                                  lambda i: (layer, row_start(i), 0, 0))
    n_in, n_out = len(in_specs), len(out_specs)
    return pl.pallas_call(
        kernel_fn, grid=(t // tm,),
        in_specs=in_specs + [pl.BlockSpec(memory_space=pl.ANY)] * 2,
        out_specs=out_specs + [cache_spec, cache_spec],
        out_shape=out_shape + [jax.ShapeDtypeStruct(c.shape, c.dtype) for c in caches],
        input_output_aliases={n_in: n_out, n_in + 1: n_out + 1},
        compiler_params=_params("parallel"), name=name)(*args, *caches)


def _proj_fox_kernel(h_ref, w_ref, wf_ref, bf_ref, *refs):
    q_ref, k_ref, v_ref, lf_ref, kc_ref, vc_ref = refs[-6:]
    hb = h_ref[...].astype(BF16)
    w = q_ref.shape[1]
    q_ref[...] = _dot(hb, w_ref[:, 0:w]).astype(BF16)
    for part, (ref, cache_ref) in enumerate(((k_ref, kc_ref), (v_ref, vc_ref)), start=1):
        z = _dot(hb, w_ref[:, part * w:(part + 1) * w])
        ref[...] = z.astype(BF16)
        _store_cache(cache_ref, z)
    fa = _dot(hb, wf_ref[...]) + bf_ref[...]
    lf = jnp.minimum(fa, 0.0) - jnp.log1p(jnp.exp(-jnp.abs(fa)))
    lf_ref[...] = lf[:, :lf_ref.shape[1]]


def _project_fox(h, w, wf, bf, layer, caches, tm, row_start):
    t, d = h.shape
    width = w.shape[1] // 3
    rows = lambda n: pl.BlockSpec((tm, n), lambda i: (i, 0))
    act = jax.ShapeDtypeStruct((t, width), BF16)
    return _cache_call(
        _proj_fox_kernel, "proj_fox", layer, caches, FOX_HEADS, row_start, t, tm,
        [rows(d), _resident(w.shape), _resident(wf.shape), _resident(bf.shape)], (h, w, wf, bf),
        [rows(width), rows(width), rows(width), rows(FOX_HEADS)],
        [act, act, act, jax.ShapeDtypeStruct((t, FOX_HEADS), F32)])


def _proj_diff_kernel(h_ref, w_ref, cs_ref, *refs):
    q_ref, k_ref, v_ref, kc_ref, vc_ref = refs[-5:]
    hb = h_ref[...].astype(BF16)
    tm, w = q_ref.shape
    cos = cs_ref[:, 0:LANES]
    sin = cs_ref[:, LANES:2 * LANES]
    half = ROT_DIM // 2
    first = lax.broadcasted_iota(jnp.int32, (tm, LANES), 1) < half

    def rope(x):
        partner = jnp.where(first, pltpu.roll(x, LANES - half, 1), pltpu.roll(x, half, 1))
        return x * cos + partner * sin

    def rotated(z):
        return jnp.concatenate([rope(z[:, g * LANES:(g + 1) * LANES]) for g in range(w // LANES)], axis=1)

    q_ref[...] = rotated(_dot(hb, w_ref[:, 0:w])).astype(BF16)
    k = rotated(_dot(hb, w_ref[:, w:2 * w]))
    k_ref[...] = k.astype(BF16)
    _store_cache(kc_ref, k)
    v = _dot(hb, w_ref[:, 2 * w:3 * w])
    v_ref[...] = v.astype(BF16)
    _store_cache(vc_ref, v)


def _project_diff(h, w, cs, layer, caches, tm, row_start):
    t, d = h.shape
    width = w.shape[1] // 3
    rows = lambda n: pl.BlockSpec((tm, n), lambda i: (i, 0))
    act = jax.ShapeDtypeStruct((t, width), BF16)
    return _cache_call(
        _proj_diff_kernel, "proj_diff", layer, caches, DIFF_HEADS, row_start, t, tm,
        [rows(d), _resident(w.shape), rows(2 * LANES)], (h, w, cs),
        [rows(width), rows(width), rows(width)], [act, act, act])


def _rope_table(pos):
    half = ROT_DIM // 2
    inv_freq = ROPE_THETA ** (-jnp.arange(half, dtype=F32) / half)
    ang = pos.astype(F32)[:, None] * inv_freq[None, :]
    cos, sin = jnp.cos(ang), jnp.sin(ang)
    n = pos.shape[0]
    ones = jnp.ones((n, LANES - ROT_DIM), F32)
    zeros = jnp.zeros((n, LANES - ROT_DIM), F32)
    return jnp.concatenate([cos, cos, ones, -sin, sin, zeros], axis=1)


def _split_bf16(x, parts):
    out, rest = [], x
    for _ in range(parts):
        t = rest.astype(BF16)
        out.append(t)
        rest = rest - t.astype(F32)
    return out


def _fox_bias_kernel(lf0_ref, lf1_ref, a0_ref, a1_ref, *, inv_scale):
    row = lax.broadcasted_iota(jnp.int32, (LANES, LANES), 0)
    col = lax.broadcasted_iota(jnp.int32, (LANES, LANES), 1)
    tri = jnp.where(col <= row, 1.0, 0.0).astype(BF16)
    spread = [jnp.where(col == BIAS_PARTS * row + t, 1.0, 0.0).astype(BF16) for t in range(BIAS_PARTS)]
    carry = jnp.zeros((1, LANES), F32)
    for lf_ref, a_ref in ((lf0_ref, a0_ref), (lf1_ref, a1_ref)):
        n = lf_ref.shape[0]
        for r0 in range(0, n, LANES):
            rows = min(LANES, n - r0)
            x = lf_ref[r0:r0 + rows]
            cs = carry
            for term in _split_bf16(x, BIAS_PARTS):
                cs = cs + _dot(tri[:rows, :rows], term)
            carry = cs[rows - 1:rows]
            a = cs * (-inv_scale)
            out = jnp.zeros((rows, LANES), F32)
            for term, e in zip(_split_bf16(a, BIAS_PARTS), spread):
                out = out + _dot(term, e)
            a_ref[r0:r0 + rows] = out.astype(BF16)


def _fox_bias(logf0, logf1):
    b, _, h = logf0.shape
    pad = lambda x: jnp.pad(x, ((0, 0), (0, 0), (0, LANES - h)))
    spec = lambda x: pl.BlockSpec((None, x.shape[1], LANES), lambda i: (i, 0, 0))
    return pl.pallas_call(
        functools.partial(_fox_bias_kernel, inv_scale=HEAD_DIM ** 0.5), grid=(b,),
        in_specs=[spec(logf0), spec(logf1)], out_specs=[spec(logf0), spec(logf1)],
        out_shape=[jax.ShapeDtypeStruct((b, x.shape[1], LANES), BF16) for x in (logf0, logf1)],
        compiler_params=_params("parallel"), name="fox_bias")(pad(logf0), pad(logf1))


def _attend(stages, c_exp, ahead=ATT_AHEAD):
    pending = {}
    for t in range(min(ahead, len(stages))):
        pending[t] = stages[t][0]()
    for t, (_, mask, (m_ref, acc_ref), slot, w, vt_blk) in enumerate(stages):
        if t + ahead < len(stages):
            pending[t + ahead] = stages[t + ahead][0]()
        st = pending.pop(t)
        if mask is not None:
            st = jnp.where(mask, st, NEG)
        m_prev = m_ref[slot, :, 0:w]
        m_new = jnp.maximum(m_prev, jnp.max(st, axis=0, keepdims=True))
        alpha = jnp.exp2((m_prev - m_new) * c_exp)
        p = jnp.exp2((st - m_new) * c_exp)
        m_ref[slot, :, 0:w] = m_new
        vt = vt_blk() if callable(vt_blk) else vt_blk
        acc_ref[slot, :, 0:w] = alpha * acc_ref[slot, :, 0:w] + _dot(vt, p.astype(BF16))


def _reset(state, slot, w):
    m_ref, acc_ref = state
    m_ref[slot, :, 0:w] = jnp.full((1, w), NEG, F32)
    acc_ref[slot, :, 0:w] = jnp.zeros((acc_ref.shape[1], w), F32)


def _normalised(state, slot, w):
    m_ref, acc_ref = state
    dv = acc_ref.shape[1] - ONES_ROWS
    return acc_ref[slot, 0:dv, 0:w] * (1.0 / acc_ref[slot, dv:dv + 1, 0:w])


def _state_scratch(slots, dv):
    return [pltpu.VMEM((slots, 1, ATT_STRIP), F32), pltpu.VMEM((slots, dv + ONES_ROWS, ATT_STRIP), F32)]


def _pad_rows(x, rows):
    return jnp.concatenate([x, jnp.zeros((rows - x.shape[0], x.shape[1]), x.dtype)], axis=0)


def _transpose_values(v_ref, vm_ref, vt, vtm):
    s_len, dv = v_ref.shape
    sw = ATT_STRIP
    vt[:, dv:, :] = jnp.ones((vt.shape[0], ONES_ROWS, sw), BF16)
    vtm[dv:, :] = jnp.ones((ONES_ROWS, LANES), BF16)
    for j in range(s_len // sw):
        for c in range(sw // LANES):
            r0 = j * sw + c * LANES
            for e in range(dv // LANES):
                vt[j, e * LANES:(e + 1) * LANES, c * LANES:(c + 1) * LANES] = (
                    v_ref[r0:r0 + LANES, e * LANES:(e + 1) * LANES].T)
    vmeta = _pad_rows(vm_ref[...], LANES)
    for e in range(dv // LANES):
        vtm[e * LANES:(e + 1) * LANES] = vmeta[:, e * LANES:(e + 1) * LANES].T


def _fox_prompt_kernel(q_ref, k_ref, v_ref, a_ref, qm_ref, km_ref, vm_ref, am_ref, o_ref, om_ref,
                       qa, ka, vt, vtm, m_ref, acc_ref, *, ns):
    s_len = q_ref.shape[0]
    sw = ATT_STRIP
    tq = ns * sw
    d = HEAD_DIM
    c_exp = d ** -0.5 * LOG2E
    state = (m_ref, acc_ref)
    lane = lax.broadcasted_iota(jnp.int32, (1, LANES), 1)
    first = pl.program_id(1) * BIAS_PARTS
    onehot = jnp.where((lane >= first) & (lane < first + BIAS_PARTS), 1.0, 0.0).astype(BF16)
    qa[0:s_len, 0:d] = q_ref[...]
    qa[s_len:, 0:d] = _pad_rows(qm_ref[...], LANES)
    qa[:, d:] = jnp.broadcast_to(onehot, (s_len + LANES, LANES))
    ka[0:s_len, 0:d] = k_ref[...]
    ka[0:s_len, d:] = a_ref[...]
    ka[s_len:, 0:d] = _pad_rows(km_ref[...], LANES)
    ka[s_len:, d:] = _pad_rows(am_ref[...], LANES)
    _transpose_values(v_ref, vm_ref, vt, vtm)
    k_meta = ka[s_len:s_len + LANES]

    kidx = lax.broadcasted_iota(jnp.int32, (LANES, LANES), 0)
    qidx = lax.broadcasted_iota(jnp.int32, (LANES, LANES), 1)
    _reset(state, 0, LANES)
    _attend([(lambda: _dot_nt(k_meta, qa[s_len:s_len + LANES]), (kidx <= qidx) & (kidx < N_META), state, 0, LANES,
              vtm[...])], c_exp)
    om_ref[...] = _normalised(state, 0, LANES).T[0:N_META].astype(om_ref.dtype)

    meta_valid = lax.broadcasted_iota(jnp.int32, (LANES, sw), 0) < N_META
    diag = lax.broadcasted_iota(jnp.int32, (sw, sw), 0) <= lax.broadcasted_iota(jnp.int32, (sw, sw), 1)

    def q_block(i, carry):
        whole = isinstance(i, int)
        r0 = i * tq if whole else pl.multiple_of(i * tq, tq)

        def scores(k_blk, c):
            return lambda: _dot_nt(k_blk, qa[pl.ds(r0 + c * sw, sw)])

        for c in range(ns):
            _reset(state, c, sw)
        stages = [(scores(k_meta, c), meta_valid, state, c, sw, vtm[...]) for c in range(ns)]
        if not whole:
            _attend(stages, c_exp, FOX_AHEAD)

            def kv_block(j, cc):
                k_blk = ka[pl.ds(pl.multiple_of(j * sw, sw), sw)]
                _attend([(scores(k_blk, c), None, state, c, sw, vt[j]) for c in range(ns)], c_exp, FOX_AHEAD)
                return cc

            lax.fori_loop(0, i * ns, kv_block, 0)
            stages = []
        for dd in range(ns):
            k_blk = ka[pl.ds(r0 + dd * sw, sw)]
            stages += [(scores(k_blk, c), diag if c == dd else None, state, c, sw, vt[i * ns + dd])
                       for c in range(dd, ns)]
        _attend(stages, c_exp, FOX_AHEAD)
        for c in range(ns):
            o_ref[pl.ds(r0 + c * sw, sw)] = _normalised(state, c, sw).T.astype(o_ref.dtype)
        return carry

    if s_len == tq:
        q_block(0, 0)
    else:
        lax.fori_loop(0, s_len // tq, q_block, 0)


def _fox_prompt(frames, meta):
    b, s_len, width = frames[0].shape
    d = HEAD_DIM
    ns = min(FOX_STRIPS, s_len // ATT_STRIP)
    head = lambda rows: pl.BlockSpec((None, rows, d), lambda i, h: (i, 0, h))
    bias = lambda rows: pl.BlockSpec((None, rows, LANES), lambda i, h: (i, 0, 0))
    group = lambda rows: [head(rows)] * 3 + [bias(rows)]
    return pl.pallas_call(
        functools.partial(_fox_prompt_kernel, ns=ns), grid=(b, width // d),
        in_specs=group(s_len) + group(N_META), out_specs=[head(s_len), head(N_META)],
        out_shape=[jax.ShapeDtypeStruct(frames[0].shape, BF16), jax.ShapeDtypeStruct(meta[0].shape, BF16)],
        scratch_shapes=[pltpu.VMEM((s_len + LANES, 2 * d), BF16), pltpu.VMEM((s_len + LANES, 2 * d), BF16),
                        pltpu.VMEM((s_len // ATT_STRIP, d + ONES_ROWS, ATT_STRIP), BF16),
                        pltpu.VMEM((d + ONES_ROWS, LANES), BF16)]
        + _state_scratch(ns, d),
        compiler_params=_params("parallel", "parallel"), name="fox_prompt")(*frames, *meta)


def _values_t(v):
    keys, dv = v.shape
    cols = [jnp.concatenate([v[c * LANES:(c + 1) * LANES, e * LANES:(e + 1) * LANES].T
                             for e in range(dv // LANES)], axis=0) for c in range(keys // LANES)]
    return jnp.concatenate([jnp.concatenate(cols, axis=1).astype(BF16), jnp.ones((ONES_ROWS, keys), BF16)], axis=0)


def _fox_sample_kernel(q_ref, kc_ref, vc_ref, ac_ref, kn_ref, vn_ref, an_ref, o_ref, qa, m_ref, acc_ref):
    j = pl.program_id(1)
    t = q_ref.shape[0]
    d = HEAD_DIM
    heads = qa.shape[0]
    tk = ac_ref.shape[0]
    c_exp = d ** -0.5 * LOG2E
    state = (m_ref, acc_ref)

    def head_rows(ref, h):
        return ref[pl.ds(h, tk, stride=heads), :]

    @pl.when(j == 0)
    def _():
        lane = lax.broadcasted_iota(jnp.int32, (1, LANES), 1)
        for h in range(heads):
            onehot = jnp.where((lane >= h * BIAS_PARTS) & (lane < (h + 1) * BIAS_PARTS), 1.0, 0.0).astype(BF16)
            qa[h, :, 0:d] = _pad_rows(q_ref[:, h * d:(h + 1) * d], LANES)
            qa[h, :, d:] = jnp.broadcast_to(onehot, (LANES, LANES))
            _reset(state, h, LANES)

    a_blk = ac_ref[...]
    _attend([(functools.partial(
        lambda h: _dot_nt(jnp.concatenate([head_rows(kc_ref, h).astype(BF16), a_blk], axis=1), qa[h]), h),
              None, state, h, LANES, functools.partial(lambda h: _values_t(head_rows(vc_ref, h)), h))
             for h in range(heads)], c_exp)

    @pl.when(j == pl.num_programs(1) - 1)
    def _():
        causal = (lax.broadcasted_iota(jnp.int32, (LANES, LANES), 0)
                  <= lax.broadcasted_iota(jnp.int32, (LANES, LANES), 1))
        a_new = an_ref[...]
        _attend([(functools.partial(
            lambda h: _dot_nt(_pad_rows(jnp.concatenate([kn_ref[:, h * d:(h + 1) * d], a_new], axis=1), LANES),
                              qa[h]), h),
                  causal, state, h, LANES,
                  functools.partial(lambda h: _values_t(_pad_rows(vn_ref[:, h * d:(h + 1) * d], LANES)), h))
                 for h in range(heads)], c_exp)
        for h in range(heads):
            o_ref[:, h * d:(h + 1) * d] = _normalised(state, h, LANES).T[0:t].astype(o_ref.dtype)


def _fox_sample(q, cache_k, cache_v, layer, a_cache, k_new, v_new, a_new):
    b, t, width = q.shape
    d = HEAD_DIM
    heads = width // d
    p = cache_k.shape[2]
    tk = min(ATT_TILE, p)
    assert t <= LANES and p % tk == 0
    new = lambda n: pl.BlockSpec((None, t, n), lambda i, j: (i, 0, 0))
    flat = lambda c: c.reshape(c.shape[0], b, p * heads, d)
    cache = pl.BlockSpec((None, None, tk * heads, d), lambda i, j: (layer, i, j, 0))
    return pl.pallas_call(
        _fox_sample_kernel, grid=(b, p // tk),
        in_specs=[new(width), cache, cache, pl.BlockSpec((None, tk, LANES), lambda i, j: (i, j, 0)),
                  new(width), new(width), new(LANES)],
        out_specs=new(width), out_shape=jax.ShapeDtypeStruct(q.shape, BF16),
        scratch_shapes=[pltpu.VMEM((heads, LANES, 2 * d), BF16)] + _state_scratch(heads, d),
        compiler_params=_params("parallel", "arbitrary"), name="fox_sample")(
            q, flat(cache_k), flat(cache_v), a_cache, k_new, v_new, a_new)


def _diff_lambda(lq1_ref, lk1_ref, lq2_ref, lk2_ref, lam_init):
    d1 = jnp.sum(lq1_ref[...] * lk1_ref[...], axis=1, keepdims=True)
    d2 = jnp.sum(lq2_ref[...] * lk2_ref[...], axis=1, keepdims=True)
    return jnp.exp(d1) - jnp.exp(d2) + lam_init


def _diff_finish_t(s1, s2, slot, w, lam, g_ref, lam_init):
    ot = _normalised(s1, slot, w) - lam * _normalised(s2, slot, w)
    ot = ot * lax.rsqrt(jnp.mean(ot * ot, axis=0, keepdims=True) + RMS_EPS)
    return ot.T * g_ref[...] * (1.0 - lam_init)


def _diff_prompt_kernel(q_ref, k_ref, v_ref, qm_ref, km_ref, vm_ref, lq1_ref, lk1_ref, lq2_ref, lk2_ref, g_ref,
                        o_ref, om_ref, vt, vtm, m1, a1, m2, a2, *, ns, lam_init):
    s_len = q_ref.shape[0]
    sw = ATT_STRIP
    tq = ns * sw
    d = HEAD_DIM
    c_exp = d ** -0.5 * LOG2E
    states = ((m1, a1), (m2, a2))
    lam = _diff_lambda(lq1_ref, lk1_ref, lq2_ref, lk2_ref, lam_init)
    _transpose_values(v_ref, vm_ref, vt, vtm)

    def both(keys, queries, c, w, mask, vt_blk):
        return [(functools.partial(lambda lo: _dot_nt(keys(lo), queries(lo)), e * d), mask, states[e], c, w, vt_blk)
                for e in range(2)]

    def meta_keys(lo):
        return _pad_rows(km_ref[:, lo:lo + d], LANES)

    def frame_rows(ref, rows):
        return lambda lo: ref[rows, lo:lo + d]

    for st in states:
        _reset(st, 0, LANES)
    real_keys = lax.broadcasted_iota(jnp.int32, (LANES, LANES), 0) < N_META
    _attend(both(meta_keys, lambda lo: _pad_rows(qm_ref[:, lo:lo + d], LANES), 0, LANES, real_keys, vtm[...]), c_exp)
    om_ref[...] = _diff_finish_t(states[0], states[1], 0, LANES, lam, g_ref, lam_init)[0:N_META].astype(om_ref.dtype)

    meta_valid = lax.broadcasted_iota(jnp.int32, (LANES, sw), 0) < N_META
    diag = (lax.broadcasted_iota(jnp.int32, (sw, sw), 0) // CHUNK
            <= lax.broadcasted_iota(jnp.int32, (sw, sw), 1) // CHUNK)

    def q_block(i, carry):
        r0 = pl.multiple_of(i * tq, tq)

        def q_rows(c):
            return pl.ds(r0 + c * sw, sw)

        for c in range(ns):
            for st in states:
                _reset(st, c, sw)
        stages = []
        for c in range(ns):
            stages += both(meta_keys, frame_rows(q_ref, q_rows(c)), c, sw, meta_valid, vtm[...])
        _attend(stages, c_exp)

        def kv_block(j, cc):
            keys = frame_rows(k_ref, pl.ds(pl.multiple_of(j * sw, sw), sw))
            stages = []
            for c in range(ns):
                stages += both(keys, frame_rows(q_ref, q_rows(c)), c, sw, None, vt[j])
            _attend(stages, c_exp)
            return cc

        lax.fori_loop(0, i * ns, kv_block, 0)
        stages = []
        for dd in range(ns):
            keys = frame_rows(k_ref, pl.ds(r0 + dd * sw, sw))
            for c in range(dd, ns):
                stages += both(keys, frame_rows(q_ref, q_rows(c)), c, sw, diag if c == dd else None, vt[i * ns + dd])
        _attend(stages, c_exp)
        for c in range(ns):
            o_ref[q_rows(c)] = _diff_finish_t(states[0], states[1], c, sw, lam, g_ref, lam_init).astype(o_ref.dtype)
        return carry

    lax.fori_loop(0, s_len // tq, q_block, 0)


def _lambda_specs():
    return [pl.BlockSpec((1, HEAD_DIM), lambda i, h: (0, 0))] * 4 + [
        pl.BlockSpec((1, 2 * HEAD_DIM), lambda i, h: (0, 0))]


def _diff_prompt(frames, meta, lam_params, g, lam_init):
    b, s_len, width = frames[0].shape
    dv = 2 * HEAD_DIM
    ns = min(DIFF_STRIPS, s_len // ATT_STRIP)
    head = lambda rows: pl.BlockSpec((None, rows, dv), lambda i, h: (i, 0, h))
    return pl.pallas_call(
        functools.partial(_diff_prompt_kernel, ns=ns, lam_init=lam_init), grid=(b, width // dv),
        in_specs=[head(s_len)] * 3 + [head(N_META)] * 3 + _lambda_specs(),
        out_specs=[head(s_len), head(N_META)],
        out_shape=[jax.ShapeDtypeStruct(frames[0].shape, BF16), jax.ShapeDtypeStruct(meta[0].shape, BF16)],
        scratch_shapes=[pltpu.VMEM((s_len // ATT_STRIP, dv + ONES_ROWS, ATT_STRIP), BF16),
                        pltpu.VMEM((dv + ONES_ROWS, LANES), BF16)]
        + _state_scratch(ns, dv) + _state_scratch(ns, dv),
        compiler_params=_params("parallel", "parallel"), name="diff_prompt")(*frames, *meta, *lam_params, g)


def _diff_sample_kernel(q_ref, kc_ref, vc_ref, kn_ref, vn_ref, lq1_ref, lk1_ref, lq2_ref, lk2_ref, g_ref, o_ref,
                        qp, m1, a1, m2, a2, *, lam_init):
    j = pl.program_id(1)
    t = q_ref.shape[0]
    d = HEAD_DIM
    heads = kc_ref.shape[1]
    c_exp = d ** -0.5 * LOG2E
    states = ((m1, a1), (m2, a2))

    @pl.when(j == 0)
    def _():
        for h in range(heads):
            for e in range(2):
                qp[2 * h + e] = _pad_rows(q_ref[:, (2 * h + e) * d:(2 * h + e + 1) * d], LANES)
                _reset(states[e], h, LANES)

    def stages(keys, values, mask):
        return [(functools.partial(lambda h, e: _dot_nt(keys(h, e), qp[2 * h + e]), h, e), mask, states[e], h, LANES,
                 functools.partial(lambda h: _values_t(values(h)), h)) for h in range(heads) for e in range(2)]

    _attend(stages(lambda h, e: kc_ref[:, h, e * d:(e + 1) * d].astype(BF16), lambda h: vc_ref[:, h, :], None), c_exp)

    @pl.when(j == pl.num_programs(1) - 1)
    def _():
        lam = _diff_lambda(lq1_ref, lk1_ref, lq2_ref, lk2_ref, lam_init)
        real = lax.broadcasted_iota(jnp.int32, (LANES, LANES), 0) < t
        _attend(stages(lambda h, e: _pad_rows(kn_ref[:, (2 * h + e) * d:(2 * h + e + 1) * d], LANES),
                       lambda h: _pad_rows(vn_ref[:, 2 * h * d:2 * (h + 1) * d], LANES), real), c_exp)
        for h in range(heads):
            o_ref[:, 2 * h * d:2 * (h + 1) * d] = _diff_finish_t(
                states[0], states[1], h, LANES, lam, g_ref, lam_init)[0:t].astype(o_ref.dtype)


def _diff_sample(q, cache_k, cache_v, layer, k_new, v_new, lam_params, g, lam_init):
    b, t, width = q.shape
    dv = 2 * HEAD_DIM
    heads = width // dv
    p = cache_k.shape[2]
    tk = min(ATT_TILE, p)
    assert t <= LANES and p % tk == 0
    new = pl.BlockSpec((None, t, width), lambda i, j: (i, 0, 0))
    cache = pl.BlockSpec((None, None, tk, heads, dv), lambda i, j: (layer, i, j, 0, 0))
    return pl.pallas_call(
        functools.partial(_diff_sample_kernel, lam_init=lam_init), grid=(b, p // tk),
        in_specs=[new, cache, cache, new, new] + _lambda_specs(),
        out_specs=new, out_shape=jax.ShapeDtypeStruct(q.shape, BF16),
        scratch_shapes=[pltpu.VMEM((2 * heads, LANES, HEAD_DIM), BF16)]
        + _state_scratch(heads, dv) + _state_scratch(heads, dv),
        compiler_params=_params("parallel", "arbitrary"), name="diff_sample")(
            q, cache_k, cache_v, k_new, v_new, *lam_params, g)


def _merge_kernel(h_ref, oa_ref, ob_ref, wga_ref, wgb_ref, wa_ref, wb_ref, m_ref):
    hb = h_ref[...].astype(BF16)
    oa, ob = oa_ref[...], ob_ref[...]
    for c in range(m_ref.shape[1] // COL_CHUNK):
        sl = slice(c * COL_CHUNK, (c + 1) * COL_CHUNK)
        ga = jax.nn.sigmoid(_dot(hb, wga_ref[:, sl]))
        gb = jax.nn.sigmoid(_dot(hb, wgb_ref[:, sl]))
        m_ref[:, sl] = (ga * _dot(oa, wa_ref[:, sl]) + gb * _dot(ob, wb_ref[:, sl])).astype(m_ref.dtype)


def _merge(h, oa, ob, wga, wgb, wa, wb):
    t, d = h.shape
    tm = min(ROW_TILE, t)
    rows = lambda n: pl.BlockSpec((tm, n), lambda i: (i, 0))
    return pl.pallas_call(
        _merge_kernel, grid=(pl.cdiv(t, tm),),
        in_specs=[rows(d), rows(oa.shape[1]), rows(ob.shape[1]),
                  _resident(wga.shape), _resident(wgb.shape), _resident(wa.shape), _resident(wb.shape)],
        out_specs=rows(d), out_shape=jax.ShapeDtypeStruct((t, d), BF16),
        compiler_params=_params("parallel"), name="gated_merge")(h, oa, ob, wga, wgb, wa, wb)


def _out_ln_kernel(h_ref, m_ref, w_ref, g_ref, b_ref, o_ref, *, alpha):
    y = _dot(m_ref[...], w_ref[...])
    o_ref[...] = _layer_norm(alpha * h_ref[...] + y, g_ref[...], b_ref[...])


def _out_proj_ln(h, m, w, g, b, alpha):
    t, d = h.shape
    tm = min(ROW_TILE, t)
    row = pl.BlockSpec((tm, d), lambda i: (i, 0))
    vec = pl.BlockSpec((1, d), lambda i: (0, 0))
    return pl.pallas_call(
        functools.partial(_out_ln_kernel, alpha=alpha), grid=(pl.cdiv(t, tm),),
        in_specs=[row, row, _resident(w.shape), vec, vec], out_specs=row,
        out_shape=jax.ShapeDtypeStruct((t, d), F32),
        compiler_params=_params("parallel"), name="out_proj_ln")(h, m, w, g.reshape(1, d), b.reshape(1, d))


def _mlp_kernel(x_ref, wu_ref, wd_ref, g_ref, b_ref, o_ref, xb_ref, acc_ref, *, alpha):
    f = pl.program_id(1)

    @pl.when(f == 0)
    def _():
        xb_ref[...] = x_ref[...].astype(BF16)
        acc_ref[...] = jnp.zeros(acc_ref.shape, F32)

    u = jnp.maximum(_dot(xb_ref[...], wu_ref[...]), 0.0)
    acc_ref[...] += _dot((u * u).astype(BF16), wd_ref[...])

    @pl.when(f == pl.num_programs(1) - 1)
    def _():
        o_ref[...] = _layer_norm(alpha * x_ref[...] + acc_ref[...], g_ref[...], b_ref[...])


def _mlp_ln(x, wu, wd, g, b, alpha):
    t, d = x.shape
    dff = wu.shape[1]
    tm = min(ROW_TILE, t)
    tf = min(FF_TILE, dff)
    row = pl.BlockSpec((tm, d), lambda i, f: (i, 0))
    vec = pl.BlockSpec((1, d), lambda i, f: (0, 0))
    return pl.pallas_call(
        functools.partial(_mlp_kernel, alpha=alpha), grid=(pl.cdiv(t, tm), dff // tf),
        in_specs=[row, pl.BlockSpec((d, tf), lambda i, f: (0, f)), pl.BlockSpec((tf, d), lambda i, f: (f, 0)),
                  vec, vec],
        out_specs=row, out_shape=jax.ShapeDtypeStruct((t, d), F32),
        scratch_shapes=[pltpu.VMEM((tm, d), BF16), pltpu.VMEM((tm, d), F32)],
        compiler_params=_params("parallel", "arbitrary"), name="mlp_ln")(
            x, wu, wd, g.reshape(1, d), b.reshape(1, d))


def kernel(x_prompt, x_sample, cache_fox_k, cache_fox_v, cache_fox_logf, cache_diff_k, cache_diff_v, meta_tokens, ln_in_g, ln_in_b, w_in, b_f, lambda_q1, lambda_k1, lambda_q2, lambda_k2, subln_g, w_br_a, w_br_b, w_out, ln1_g, ln1_b, w_up, w_down, ln2_g, ln2_b):
    b, s_len, d = x_prompt.shape
    bd, t_new, _ = x_sample.shape
    depth = w_in.shape[0]
    p = cache_fox_k.shape[2]
    n = N_META + s_len
    alpha = (2 * depth) ** 0.25
    fox_w = FOX_HEADS * HEAD_DIM
    diff_w = DIFF_HEADS * 2 * HEAD_DIM

    o_f = 3 * fox_w
    o_b = o_f + FOX_HEADS
    o_g = o_b + 3 * diff_w
    w_in_b = w_in.astype(BF16)
    w_fox = w_in_b[:, :, :o_f]
    w_f = jnp.pad(w_in_b[:, :, o_f:o_b], ((0, 0), (0, 0), (0, LANES - FOX_HEADS)))
    w_diff = w_in_b[:, :, o_b:o_g]
    w_ga = w_in_b[:, :, o_g:o_g + d]
    w_gb = w_in_b[:, :, o_g + d:]
    b_f_pad = jnp.pad(b_f.astype(F32), ((0, 0), (0, LANES - FOX_HEADS)))[:, None, :]
    w_a, w_b, w_o = w_br_a.astype(BF16), w_br_b.astype(BF16), w_out.astype(BF16)
    w_u, w_d = w_up.astype(BF16), w_down.astype(BF16)

    cs_f = jnp.tile(_rope_table(N_META + jnp.arange(s_len)), (b, 1))
    cs_m = jnp.tile(_rope_table(jnp.arange(N_META)), (b, 1))
    cs_s = jnp.tile(_rope_table(p + jnp.arange(t_new)), (bd, 1))

    tm_f = min(ROW_TILE, s_len)
    tm_s = min(ROW_TILE, bd * t_new)
    assert s_len % tm_f == 0 and (bd * t_new) % tm_s == 0
    blocks_per_seq = s_len // tm_f
    frame_rows = lambda i: (i // blocks_per_seq) * n + N_META + (i % blocks_per_seq) * tm_f
    meta_rows = lambda i: i * n
    sample_rows = lambda i: i * tm_s

    hf = _input_layer_norm(x_prompt.reshape(b * s_len, d), ln_in_g, ln_in_b)
    hm = jnp.tile(_input_layer_norm(meta_tokens.astype(F32), ln_in_g, ln_in_b), (b, 1))
    hs = _input_layer_norm(x_sample.reshape(bd * t_new, d), ln_in_g, ln_in_b)

    def empty_caches(rows, heads, dim):
        shape = (depth, rows * heads, dim) if dim == LANES else (depth, rows, heads, dim)
        return [jnp.zeros(shape, F32) for _ in range(2)]

    fox_p, diff_p = empty_caches(b * n, FOX_HEADS, HEAD_DIM), empty_caches(b * n, DIFF_HEADS, 2 * HEAD_DIM)
    fox_s, diff_s = empty_caches(bd * t_new, FOX_HEADS, HEAD_DIM), empty_caches(bd * t_new, DIFF_HEADS, 2 * HEAD_DIM)
    logf_p, logf_s = [], []
    for l in range(depth):
        lam_init = 0.8 - 0.6 * math.exp(-0.3 * l)
        lam_params = [a[l].reshape(1, HEAD_DIM).astype(F32) for a in (lambda_q1, lambda_k1, lambda_q2, lambda_k2)]
        g_sub = subln_g[l].reshape(1, 2 * HEAD_DIM).astype(F32)

        def project(x, cs, fox_c, diff_c, tm, row_start):
            qa, ka, va, lf, *fox_c = _project_fox(x, w_fox[l], w_f[l], b_f_pad[l], l, fox_c, tm, row_start)
            qb, kb, vb, *diff_c = _project_diff(x, w_diff[l], cs, l, diff_c, tm, row_start)
            return (qa, ka, va), lf, (qb, kb, vb), fox_c, diff_c

        def channel_mix(x, oa, ob):
            m = _merge(x, oa.reshape(-1, fox_w), ob.reshape(-1, diff_w), w_ga[l], w_gb[l], w_a[l], w_b[l])
            x = _out_proj_ln(x, m, w_o[l], ln1_g[l], ln1_b[l], alpha)
            return _mlp_ln(x, w_u[l], w_d[l], ln2_g[l], ln2_b[l], alpha)

        def per_seq(xs, batch):
            return tuple(x.reshape(batch, x.shape[0] // batch, x.shape[-1]) for x in xs)

        qkv_a, lf_f, qkv_b, fox_p, diff_p = project(hf, cs_f, fox_p, diff_p, tm_f, frame_rows)
        qkv_am, lf_m, qkv_bm, fox_p, diff_p = project(hm, cs_m, fox_p, diff_p, N_META, meta_rows)
        lf_f, lf_m = lf_f.reshape(b, s_len, FOX_HEADS), lf_m.reshape(b, N_META, FOX_HEADS)
        logf_p.append(jnp.concatenate([lf_m, lf_f], axis=1))
        a_m, a_f = _fox_bias(lf_m, lf_f)
        oa_f, oa_m = _fox_prompt(per_seq(qkv_a, b) + (a_f,), per_seq(qkv_am, b) + (a_m,))
        ob_f, ob_m = _diff_prompt(per_seq(qkv_b, b), per_seq(qkv_bm, b), lam_params, g_sub, lam_init)
        hf = channel_mix(hf, oa_f, ob_f)
        hm = channel_mix(hm, oa_m, ob_m)

        qkv_a, lf_s, qkv_b, fox_s, diff_s = project(hs, cs_s, fox_s, diff_s, tm_s, sample_rows)
        lf_s = lf_s.reshape(bd, t_new, FOX_HEADS)
        logf_s.append(lf_s)
        a_c, a_n = _fox_bias(cache_fox_logf[l].astype(F32), lf_s)
        qa, ka, va = per_seq(qkv_a, bd)
        qb, kb, vb = per_seq(qkv_b, bd)
        oa_s = _fox_sample(qa, cache_fox_k, cache_fox_v, l, a_c, ka, va, a_n)
        ob_s = _diff_sample(qb, cache_diff_k, cache_diff_v, l, kb, vb, lam_params, g_sub, lam_init)
        hs = channel_mix(hs, oa_s, ob_s)

    def per_layer_seq(caches, batch, rows, heads):
        return tuple(c.reshape(depth, batch, rows, heads, c.shape[-1]) for c in caches)

    fk_p, fv_p = per_layer_seq(fox_p, b, n, FOX_HEADS)
    dk_p, dv_p = per_layer_seq(diff_p, b, n, DIFF_HEADS)
    fk_s, fv_s = per_layer_seq(fox_s, bd, t_new, FOX_HEADS)
    dk_s, dv_s = per_layer_seq(diff_s, bd, t_new, DIFF_HEADS)
    return (hf.reshape(b, s_len, d), hs.reshape(bd, t_new, d), fk_p, fv_p, jnp.stack(logf_p), dk_p, dv_p,
            fk_s, fv_s, jnp.stack(logf_s), dk_s, dv_s)
```

```python
import functools
import math

import jax
import jax.numpy as jnp
from jax import lax
from jax.experimental import pallas as pl
from jax.experimental.pallas import tpu as pltpu

N_META = 16
CHUNK = 64
FOX_HEADS = 8
DIFF_HEADS = 4
HEAD_DIM = 128
ROT_DIM = 32
ROPE_THETA = 500000.0
LN_EPS = 1e-5
RMS_EPS = 1e-5
NEG = -1e30

LANES = 128
VMEM_LIMIT_BYTES = 56 * 2**20

ROW_TILE = 512
ATT_TILE = 512
FF_TILE = 1024
COL_CHUNK = 512
ATT_STRIP = 256
FOX_STRIPS = 16
DIFF_STRIPS = 8
ATT_AHEAD = 6
FOX_AHEAD = 8
ONES_ROWS = 16
BIAS_PARTS = 3
LOG2E = math.log2(math.e)
KEY_SCALE = HEAD_DIM ** -0.5 * LOG2E

F32 = jnp.float32
BF16 = jnp.bfloat16


def _params(*sem):
    return pltpu.CompilerParams(dimension_semantics=sem, vmem_limit_bytes=VMEM_LIMIT_BYTES)


def _resident(shape):
    return pl.BlockSpec(shape, lambda *_: (0,) * len(shape), pipeline_mode=pl.Buffered(1))


def _layer_norm(x, g, b):
    mu = jnp.mean(x, axis=-1, keepdims=True)
    xc = x - mu
    var = jnp.mean(xc * xc, axis=-1, keepdims=True)
    return xc * lax.rsqrt(var + LN_EPS) * g + b


def _dot(a, b):
    return jnp.dot(a, b, preferred_element_type=F32)


def _dot_nt(a, b):
    return lax.dot_general(a, b, (((1,), (1,)), ((), ())), preferred_element_type=F32)


def _ln_kernel(x_ref, g_ref, b_ref, o_ref):
    o_ref[...] = _layer_norm(x_ref[...], g_ref[...], b_ref[...])


def _input_layer_norm(x, g, b):
    t, d = x.shape
    tm = min(ROW_TILE, t)
    row = pl.BlockSpec((tm, d), lambda i: (i, 0))
    vec = pl.BlockSpec((1, d), lambda i: (0, 0))
    return pl.pallas_call(
        _ln_kernel, grid=(pl.cdiv(t, tm),), in_specs=[row, vec, vec], out_specs=row,
        out_shape=jax.ShapeDtypeStruct((t, d), F32), compiler_params=_params("parallel"),
        name="input_ln")(x, g.reshape(1, d), b.reshape(1, d))


def _store_cache(cache_ref, x):
    rows, dim = x.shape[0], cache_ref.shape[-1]
    heads = x.shape[1] // dim
    for head in range(heads):
        part = x[:, head * dim:(head + 1) * dim]
        if len(cache_ref.shape) == 3:
            cache_ref[0, pl.ds(head, rows, stride=heads), :] = part
        else:
            cache_ref[0, :, head, :] = part


def _cache_call(kernel_fn, name, layer, caches, heads, row_start, t, tm, in_specs, args, out_specs, out_shape):
    dim = caches[0].shape[-1]
    if caches[0].ndim == 3:
        cache_spec = pl.BlockSpec((pl.Element(1), pl.Element(tm * heads), pl.Element(dim)),
                                  lambda i: (layer, pl.multiple_of(row_start(i) * heads, heads), 0))
    else:
        cache_spec = pl.BlockSpec((pl.Element(1), pl.Element(tm), pl.Element(heads), pl.Element(dim)),
                                  lambda i: (layer, row_start(i), 0, 0))
    n_in, n_out = len(in_specs), len(out_specs)
    return pl.pallas_call(
        kernel_fn, grid=(t // tm,),
        in_specs=in_specs + [pl.BlockSpec(memory_space=pl.ANY)] * 2,
        out_specs=out_specs + [cache_spec, cache_spec],
        out_shape=out_shape + [jax.ShapeDtypeStruct(c.shape, c.dtype) for c in caches],
        input_output_aliases={n_in: n_out, n_in + 1: n_out + 1},
        compiler_params=_params("parallel"), name=name)(*args, *caches)


def _proj_fox_kernel(h_ref, w_ref, wf_ref, bf_ref, *refs):
    q_ref, k_ref, v_ref, lf_ref, kc_ref, vc_ref = refs[-6:]
    hb = h_ref[...].astype(BF16)
    w = q_ref.shape[1]
    q_ref[...] = _dot(hb, w_ref[:, 0:w]).astype(BF16)
    for part, (ref, cache_ref, operand_scale) in enumerate(((k_ref, kc_ref, KEY_SCALE), (v_ref, vc_ref, 1.0)), start=1):
        z = _dot(hb, w_ref[:, part * w:(part + 1) * w])
        ref[...] = (z if operand_scale == 1.0 else z * operand_scale).astype(BF16)
        _store_cache(cache_ref, z)
    fa = _dot(hb, wf_ref[...]) + bf_ref[...]
    lf = jnp.minimum(fa, 0.0) - jnp.log1p(jnp.exp(-jnp.abs(fa)))
    lf_ref[...] = lf[:, :lf_ref.shape[1]]


def _project_fox(h, w, wf, bf, layer, caches, tm, row_start):
    t, d = h.shape
    width = w.shape[1] // 3
    rows = lambda n: pl.BlockSpec((tm, n), lambda i: (i, 0))
    act = jax.ShapeDtypeStruct((t, width), BF16)
    return _cache_call(
        _proj_fox_kernel, "proj_fox", layer, caches, FOX_HEADS, row_start, t, tm,
        [rows(d), _resident(w.shape), _resident(wf.shape), _resident(bf.shape)], (h, w, wf, bf),
        [rows(width), rows(width), rows(width), rows(FOX_HEADS)],
        [act, act, act, jax.ShapeDtypeStruct((t, FOX_HEADS), F32)])


def _proj_diff_kernel(h_ref, w_ref, cs_ref, *refs):
    q_ref, k_ref, v_ref, kc_ref, vc_ref = refs[-5:]
    hb = h_ref[...].astype(BF16)
    tm, w = q_ref.shape
    cos = cs_ref[:, 0:LANES]
    sin = cs_ref[:, LANES:2 * LANES]
    half = ROT_DIM // 2
    first = lax.broadcasted_iota(jnp.int32, (tm, LANES), 1) < half

    def rope(x):
        partner = jnp.where(first, pltpu.roll(x, LANES - half, 1), pltpu.roll(x, half, 1))
        return x * cos + partner * sin

    def rotated(z):
        return jnp.concatenate([rope(z[:, g * LANES:(g + 1) * LANES]) for g in range(w // LANES)], axis=1)

    q_ref[...] = rotated(_dot(hb, w_ref[:, 0:w])).astype(BF16)
    k = rotated(_dot(hb, w_ref[:, w:2 * w]))
    k_ref[...] = (k * KEY_SCALE).astype(BF16)
    _store_cache(kc_ref, k)
    v = _dot(hb, w_ref[:, 2 * w:3 * w])
    v_ref[...] = v.astype(BF16)
    _store_cache(vc_ref, v)


def _project_diff(h, w, cs, layer, caches, tm, row_start):
    t, d = h.shape
    width = w.shape[1] // 3
    rows = lambda n: pl.BlockSpec((tm, n), lambda i: (i, 0))
    act = jax.ShapeDtypeStruct((t, width), BF16)
    return _cache_call(
        _proj_diff_kernel, "proj_diff", layer, caches, DIFF_HEADS, row_start, t, tm,
        [rows(d), _resident(w.shape), rows(2 * LANES)], (h, w, cs),
        [rows(width), rows(width), rows(width)], [act, act, act])


def _rope_table(pos):
    half = ROT_DIM // 2
    inv_freq = ROPE_THETA ** (-jnp.arange(half, dtype=F32) / half)
    ang = pos.astype(F32)[:, None] * inv_freq[None, :]
    cos, sin = jnp.cos(ang), jnp.sin(ang)
    n = pos.shape[0]
    ones = jnp.ones((n, LANES - ROT_DIM), F32)
    zeros = jnp.zeros((n, LANES - ROT_DIM), F32)
    return jnp.concatenate([cos, cos, ones, -sin, sin, zeros], axis=1)


def _split_bf16(x, parts):
    out, rest = [], x
    for _ in range(parts):
        t = rest.astype(BF16)
        out.append(t)
        rest = rest - t.astype(F32)
    return out


def _fox_bias_kernel(lf0_ref, lf1_ref, a0_ref, a1_ref):
    row = lax.broadcasted_iota(jnp.int32, (LANES, LANES), 0)
    col = lax.broadcasted_iota(jnp.int32, (LANES, LANES), 1)
    tri = jnp.where(col <= row, 1.0, 0.0).astype(BF16)
    spread = [jnp.where(col == BIAS_PARTS * row + t, 1.0, 0.0).astype(BF16) for t in range(BIAS_PARTS)]
    carry = jnp.zeros((1, LANES), F32)
    for lf_ref, a_ref in ((lf0_ref, a0_ref), (lf1_ref, a1_ref)):
        n = lf_ref.shape[0]
        for r0 in range(0, n, LANES):
            rows = min(LANES, n - r0)
            x = lf_ref[r0:r0 + rows]
            cs = carry
            for term in _split_bf16(x, BIAS_PARTS):
                cs = cs + _dot(tri[:rows, :rows], term)
            carry = cs[rows - 1:rows]
            a = cs * (-LOG2E)
            out = jnp.zeros((rows, LANES), F32)
            for term, e in zip(_split_bf16(a, BIAS_PARTS), spread):
                out = out + _dot(term, e)
            a_ref[r0:r0 + rows] = out.astype(BF16)


def _fox_bias(logf0, logf1):
    b, _, h = logf0.shape
    pad = lambda x: jnp.pad(x, ((0, 0), (0, 0), (0, LANES - h)))
    spec = lambda x: pl.BlockSpec((None, x.shape[1], LANES), lambda i: (i, 0, 0))
    return pl.pallas_call(
        _fox_bias_kernel, grid=(b,),
        in_specs=[spec(logf0), spec(logf1)], out_specs=[spec(logf0), spec(logf1)],
        out_shape=[jax.ShapeDtypeStruct((b, x.shape[1], LANES), BF16) for x in (logf0, logf1)],
        compiler_params=_params("parallel"), name="fox_bias")(pad(logf0), pad(logf1))


def _attend(stages, ahead=ATT_AHEAD):
    pending = {}
    for t in range(min(ahead, len(stages))):
        pending[t] = stages[t][0]()
    for t, (_, mask, (m_ref, acc_ref), slot, w, vt_blk) in enumerate(stages):
        if t + ahead < len(stages):
            pending[t + ahead] = stages[t + ahead][0]()
        st = pending.pop(t)
        if mask is not None:
            st = jnp.where(mask, st, NEG)
        m_prev = m_ref[slot, :, 0:w]
        m_new = jnp.maximum(m_prev, jnp.max(st, axis=0, keepdims=True))
        alpha = jnp.exp2(m_prev - m_new)
        p = jnp.exp2(st - m_new)
        m_ref[slot, :, 0:w] = m_new
        vt = vt_blk() if callable(vt_blk) else vt_blk
        acc_ref[slot, :, 0:w] = alpha * acc_ref[slot, :, 0:w] + _dot(vt, p.astype(BF16))


def _reset(state, slot, w):
    m_ref, acc_ref = state
    m_ref[slot, :, 0:w] = jnp.full((1, w), NEG, F32)
    acc_ref[slot, :, 0:w] = jnp.zeros((acc_ref.shape[1], w), F32)


def _normalised(state, slot, w):
    m_ref, acc_ref = state
    dv = acc_ref.shape[1] - ONES_ROWS
    return acc_ref[slot, 0:dv, 0:w] * (1.0 / acc_ref[slot, dv:dv + 1, 0:w])


def _state_scratch(slots, dv):
    return [pltpu.VMEM((slots, 1, ATT_STRIP), F32), pltpu.VMEM((slots, dv + ONES_ROWS, ATT_STRIP), F32)]


def _pad_rows(x, rows):
    return jnp.concatenate([x, jnp.zeros((rows - x.shape[0], x.shape[1]), x.dtype)], axis=0)


def _transpose_values(v_ref, vm_ref, vt, vtm):
    s_len, dv = v_ref.shape
    sw = ATT_STRIP
    vt[:, dv:, :] = jnp.ones((vt.shape[0], ONES_ROWS, sw), BF16)
    vtm[dv:, :] = jnp.ones((ONES_ROWS, LANES), BF16)
    for j in range(s_len // sw):
        for c in range(sw // LANES):
            r0 = j * sw + c * LANES
            for e in range(dv // LANES):
                vt[j, e * LANES:(e + 1) * LANES, c * LANES:(c + 1) * LANES] = (
                    v_ref[r0:r0 + LANES, e * LANES:(e + 1) * LANES].T)
    vmeta = _pad_rows(vm_ref[...], LANES)
    for e in range(dv // LANES):
        vtm[e * LANES:(e + 1) * LANES] = vmeta[:, e * LANES:(e + 1) * LANES].T


def _fox_prompt_kernel(q_ref, k_ref, v_ref, a_ref, qm_ref, km_ref, vm_ref, am_ref, o_ref, om_ref,
                       qa, ka, vt, vtm, m_ref, acc_ref, *, ns):
    s_len = q_ref.shape[0]
    sw = ATT_STRIP
    tq = ns * sw
    d = HEAD_DIM
    state = (m_ref, acc_ref)
    lane = lax.broadcasted_iota(jnp.int32, (1, LANES), 1)
    first = pl.program_id(1) * BIAS_PARTS
    onehot = jnp.where((lane >= first) & (lane < first + BIAS_PARTS), 1.0, 0.0).astype(BF16)
    qa[0:s_len, 0:d] = q_ref[...]
    qa[s_len:, 0:d] = _pad_rows(qm_ref[...], LANES)
    qa[:, d:] = jnp.broadcast_to(onehot, (s_len + LANES, LANES))
    ka[0:s_len, 0:d] = k_ref[...]
    ka[0:s_len, d:] = a_ref[...]
    ka[s_len:, 0:d] = _pad_rows(km_ref[...], LANES)
    ka[s_len:, d:] = _pad_rows(am_ref[...], LANES)
    _transpose_values(v_ref, vm_ref, vt, vtm)
    k_meta = ka[s_len:s_len + LANES]

    kidx = lax.broadcasted_iota(jnp.int32, (LANES, LANES), 0)
    qidx = lax.broadcasted_iota(jnp.int32, (LANES, LANES), 1)
    _reset(state, 0, LANES)
    _attend([(lambda: _dot_nt(k_meta, qa[s_len:s_len + LANES]), (kidx <= qidx) & (kidx < N_META), state, 0, LANES,
              vtm[...])])
    om_ref[...] = _normalised(state, 0, LANES).T[0:N_META].astype(om_ref.dtype)

    meta_valid = lax.broadcasted_iota(jnp.int32, (LANES, sw), 0) < N_META
    diag = lax.broadcasted_iota(jnp.int32, (sw, sw), 0) <= lax.broadcasted_iota(jnp.int32, (sw, sw), 1)

    def q_block(i, carry):
        whole = isinstance(i, int)
        r0 = i * tq if whole else pl.multiple_of(i * tq, tq)

        def scores(k_blk, c):
            return lambda: _dot_nt(k_blk, qa[pl.ds(r0 + c * sw, sw)])

        for c in range(ns):
            _reset(state, c, sw)
        stages = [(scores(k_meta, c), meta_valid, state, c, sw, vtm[...]) for c in range(ns)]
        if not whole:
            _attend(stages, FOX_AHEAD)

            def kv_block(j, cc):
                k_blk = ka[pl.ds(pl.multiple_of(j * sw, sw), sw)]
                _attend([(scores(k_blk, c), None, state, c, sw, vt[j]) for c in range(ns)], FOX_AHEAD)
                return cc

            lax.fori_loop(0, i * ns, kv_block, 0)
            stages = []
        for dd in range(ns):
            k_blk = ka[pl.ds(r0 + dd * sw, sw)]
            stages += [(scores(k_blk, c), diag if c == dd else None, state, c, sw, vt[i * ns + dd])
                       for c in range(dd, ns)]
        _attend(stages, FOX_AHEAD)
        for c in range(ns):
            o_ref[pl.ds(r0 + c * sw, sw)] = _normalised(state, c, sw).T.astype(o_ref.dtype)
        return carry

    if s_len == tq:
        q_block(0, 0)
    else:
        lax.fori_loop(0, s_len // tq, q_block, 0)


def _fox_prompt(frames, meta):
    b, s_len, width = frames[0].shape
    d = HEAD_DIM
    ns = min(FOX_STRIPS, s_len // ATT_STRIP)
    head = lambda rows: pl.BlockSpec((None, rows, d), lambda i, h: (i, 0, h))
    bias = lambda rows: pl.BlockSpec((None, rows, LANES), lambda i, h: (i, 0, 0))
    group = lambda rows: [head(rows)] * 3 + [bias(rows)]
    return pl.pallas_call(
        functools.partial(_fox_prompt_kernel, ns=ns), grid=(b, width // d),
        in_specs=group(s_len) + group(N_META), out_specs=[head(s_len), head(N_META)],
        out_shape=[jax.ShapeDtypeStruct(frames[0].shape, BF16), jax.ShapeDtypeStruct(meta[0].shape, BF16)],
        scratch_shapes=[pltpu.VMEM((s_len + LANES, 2 * d), BF16), pltpu.VMEM((s_len + LANES, 2 * d), BF16),
                        pltpu.VMEM((s_len // ATT_STRIP, d + ONES_ROWS, ATT_STRIP), BF16),
                        pltpu.VMEM((d + ONES_ROWS, LANES), BF16)]
        + _state_scratch(ns, d),
        compiler_params=_params("parallel", "parallel"), name="fox_prompt")(*frames, *meta)


def _values_t(v):
    keys, dv = v.shape
    cols = [jnp.concatenate([v[c * LANES:(c + 1) * LANES, e * LANES:(e + 1) * LANES].T
                             for e in range(dv // LANES)], axis=0) for c in range(keys // LANES)]
    return jnp.concatenate([jnp.concatenate(cols, axis=1).astype(BF16), jnp.ones((ONES_ROWS, keys), BF16)], axis=0)


def _fox_sample_kernel(q_ref, kc_ref, vc_ref, ac_ref, kn_ref, vn_ref, an_ref, o_ref, qa, m_ref, acc_ref):
    j = pl.program_id(1)
    t = q_ref.shape[0]
    d = HEAD_DIM
    heads = qa.shape[0]
    tk = ac_ref.shape[0]
    state = (m_ref, acc_ref)

    def head_rows(ref, h):
        return ref[pl.ds(h, tk, stride=heads), :]

    @pl.when(j == 0)
    def _():
        lane = lax.broadcasted_iota(jnp.int32, (1, LANES), 1)
        for h in range(heads):
            onehot = jnp.where((lane >= h * BIAS_PARTS) & (lane < (h + 1) * BIAS_PARTS), 1.0, 0.0).astype(BF16)
            qa[h, :, 0:d] = _pad_rows(q_ref[:, h * d:(h + 1) * d], LANES)
            qa[h, :, d:] = jnp.broadcast_to(onehot, (LANES, LANES))
            _reset(state, h, LANES)

    a_blk = ac_ref[...]
    _attend([(functools.partial(
        lambda h: _dot_nt(jnp.concatenate([(head_rows(kc_ref, h) * KEY_SCALE).astype(BF16), a_blk], axis=1),
                          qa[h]), h),
              None, state, h, LANES, functools.partial(lambda h: _values_t(head_rows(vc_ref, h)), h))
             for h in range(heads)])

    @pl.when(j == pl.num_programs(1) - 1)
    def _():
        causal = (lax.broadcasted_iota(jnp.int32, (LANES, LANES), 0)
                  <= lax.broadcasted_iota(jnp.int32, (LANES, LANES), 1))
        a_new = an_ref[...]
        _attend([(functools.partial(
            lambda h: _dot_nt(_pad_rows(jnp.concatenate([kn_ref[:, h * d:(h + 1) * d], a_new], axis=1), LANES),
                              qa[h]), h),
                  causal, state, h, LANES,
                  functools.partial(lambda h: _values_t(_pad_rows(vn_ref[:, h * d:(h + 1) * d], LANES)), h))
                 for h in range(heads)])
        for h in range(heads):
            o_ref[:, h * d:(h + 1) * d] = _normalised(state, h, LANES).T[0:t].astype(o_ref.dtype)


def _fox_sample(q, cache_k, cache_v, layer, a_cache, k_new, v_new, a_new):
    b, t, width = q.shape
    d = HEAD_DIM
    heads = width // d
    p = cache_k.shape[2]
    tk = min(ATT_TILE, p)
    assert t <= LANES and p % tk == 0
    new = lambda n: pl.BlockSpec((None, t, n), lambda i, j: (i, 0, 0))
    flat = lambda c: c.reshape(c.shape[0], b, p * heads, d)
    cache = pl.BlockSpec((None, None, tk * heads, d), lambda i, j: (layer, i, j, 0))
    return pl.pallas_call(
        _fox_sample_kernel, grid=(b, p // tk),
        in_specs=[new(width), cache, cache, pl.BlockSpec((None, tk, LANES), lambda i, j: (i, j, 0)),
                  new(width), new(width), new(LANES)],
        out_specs=new(width), out_shape=jax.ShapeDtypeStruct(q.shape, BF16),
        scratch_shapes=[pltpu.VMEM((heads, LANES, 2 * d), BF16)] + _state_scratch(heads, d),
        compiler_params=_params("parallel", "arbitrary"), name="fox_sample")(
            q, flat(cache_k), flat(cache_v), a_cache, k_new, v_new, a_new)


def _diff_lambda(lq1_ref, lk1_ref, lq2_ref, lk2_ref, lam_init):
    d1 = jnp.sum(lq1_ref[...] * lk1_ref[...], axis=1, keepdims=True)
    d2 = jnp.sum(lq2_ref[...] * lk2_ref[...], axis=1, keepdims=True)
    return jnp.exp(d1) - jnp.exp(d2) + lam_init


def _diff_finish_t(s1, s2, slot, w, lam, g_ref, lam_init):
    ot = _normalised(s1, slot, w) - lam * _normalised(s2, slot, w)
    ot = ot * lax.rsqrt(jnp.mean(ot * ot, axis=0, keepdims=True) + RMS_EPS)
    return ot.T * g_ref[...] * (1.0 - lam_init)


def _diff_prompt_kernel(q_ref, k_ref, v_ref, qm_ref, km_ref, vm_ref, lq1_ref, lk1_ref, lq2_ref, lk2_ref, g_ref,
                        o_ref, om_ref, vt, vtm, m1, a1, m2, a2, *, ns, lam_init):
    s_len = q_ref.shape[0]
    sw = ATT_STRIP
    tq = ns * sw
    d = HEAD_DIM
    states = ((m1, a1), (m2, a2))
    lam = _diff_lambda(lq1_ref, lk1_ref, lq2_ref, lk2_ref, lam_init)
    _transpose_values(v_ref, vm_ref, vt, vtm)

    def both(keys, queries, c, w, mask, vt_blk):
        return [(functools.partial(lambda lo: _dot_nt(keys(lo), queries(lo)), e * d), mask, states[e], c, w, vt_blk)
                for e in range(2)]

    def meta_keys(lo):
        return _pad_rows(km_ref[:, lo:lo + d], LANES)

    def frame_rows(ref, rows):
        return lambda lo: ref[rows, lo:lo + d]

    for st in states:
        _reset(st, 0, LANES)
    real_keys = lax.broadcasted_iota(jnp.int32, (LANES, LANES), 0) < N_META
    _attend(both(meta_keys, lambda lo: _pad_rows(qm_ref[:, lo:lo + d], LANES), 0, LANES, real_keys, vtm[...]))
    om_ref[...] = _diff_finish_t(states[0], states[1], 0, LANES, lam, g_ref, lam_init)[0:N_META].astype(om_ref.dtype)

    meta_valid = lax.broadcasted_iota(jnp.int32, (LANES, sw), 0) < N_META
    diag = (lax.broadcasted_iota(jnp.int32, (sw, sw), 0) // CHUNK
            <= lax.broadcasted_iota(jnp.int32, (sw, sw), 1) // CHUNK)

    def q_block(i, carry):
        r0 = pl.multiple_of(i * tq, tq)

        def q_rows(c):
            return pl.ds(r0 + c * sw, sw)

        for c in range(ns):
            for st in states:
                _reset(st, c, sw)
        stages = []
        for c in range(ns):
            stages += both(meta_keys, frame_rows(q_ref, q_rows(c)), c, sw, meta_valid, vtm[...])
        _attend(stages)

        def kv_block(j, cc):
            keys = frame_rows(k_ref, pl.ds(pl.multiple_of(j * sw, sw), sw))
            stages = []
            for c in range(ns):
                stages += both(keys, frame_rows(q_ref, q_rows(c)), c, sw, None, vt[j])
            _attend(stages)
            return cc

        lax.fori_loop(0, i * ns, kv_block, 0)
        stages = []
        for dd in range(ns):
            keys = frame_rows(k_ref, pl.ds(r0 + dd * sw, sw))
            for c in range(dd, ns):
                stages += both(keys, frame_rows(q_ref, q_rows(c)), c, sw, diag if c == dd else None, vt[i * ns + dd])
        _attend(stages)
        for c in range(ns):
            o_ref[q_rows(c)] = _diff_finish_t(states[0], states[1], c, sw, lam, g_ref, lam_init).astype(o_ref.dtype)
        return carry

    lax.fori_loop(0, s_len // tq, q_block, 0)


def _lambda_specs():
    return [pl.BlockSpec((1, HEAD_DIM), lambda i, h: (0, 0))] * 4 + [
        pl.BlockSpec((1, 2 * HEAD_DIM), lambda i, h: (0, 0))]


def _diff_prompt(frames, meta, lam_params, g, lam_init):
    b, s_len, width = frames[0].shape
    dv = 2 * HEAD_DIM
    ns = min(DIFF_STRIPS, s_len // ATT_STRIP)
    head = lambda rows: pl.BlockSpec((None, rows, dv), lambda i, h: (i, 0, h))
    return pl.pallas_call(
        functools.partial(_diff_prompt_kernel, ns=ns, lam_init=lam_init), grid=(b, width // dv),
        in_specs=[head(s_len)] * 3 + [head(N_META)] * 3 + _lambda_specs(),
        out_specs=[head(s_len), head(N_META)],
        out_shape=[jax.ShapeDtypeStruct(frames[0].shape, BF16), jax.ShapeDtypeStruct(meta[0].shape, BF16)],
        scratch_shapes=[pltpu.VMEM((s_len // ATT_STRIP, dv + ONES_ROWS, ATT_STRIP), BF16),
                        pltpu.VMEM((dv + ONES_ROWS, LANES), BF16)]
        + _state_scratch(ns, dv) + _state_scratch(ns, dv),
        compiler_params=_params("parallel", "parallel"), name="diff_prompt")(*frames, *meta, *lam_params, g)


def _diff_sample_kernel(q_ref, kc_ref, vc_ref, kn_ref, vn_ref, lq1_ref, lk1_ref, lq2_ref, lk2_ref, g_ref, o_ref,
                        qp, m1, a1, m2, a2, *, lam_init):
    j = pl.program_id(1)
    t = q_ref.shape[0]
    d = HEAD_DIM
    heads = kc_ref.shape[1]
    states = ((m1, a1), (m2, a2))

    @pl.when(j == 0)
    def _():
        for h in range(heads):
            for e in range(2):
                qp[2 * h + e] = _pad_rows(q_ref[:, (2 * h + e) * d:(2 * h + e + 1) * d], LANES)
                _reset(states[e], h, LANES)

    def stages(keys, values, mask):
        return [(functools.partial(lambda h, e: _dot_nt(keys(h, e), qp[2 * h + e]), h, e), mask, states[e], h, LANES,
                 functools.partial(lambda h: _values_t(values(h)), h)) for h in range(heads) for e in range(2)]

    _attend(stages(lambda h, e: (kc_ref[:, h, e * d:(e + 1) * d] * KEY_SCALE).astype(BF16),
                   lambda h: vc_ref[:, h, :], None))

    @pl.when(j == pl.num_programs(1) - 1)
    def _():
        lam = _diff_lambda(lq1_ref, lk1_ref, lq2_ref, lk2_ref, lam_init)
        real = lax.broadcasted_iota(jnp.int32, (LANES, LANES), 0) < t
        _attend(stages(lambda h, e: _pad_rows(kn_ref[:, (2 * h + e) * d:(2 * h + e + 1) * d], LANES),
                       lambda h: _pad_rows(vn_ref[:, 2 * h * d:2 * (h + 1) * d], LANES), real))
        for h in range(heads):
            o_ref[:, 2 * h * d:2 * (h + 1) * d] = _diff_finish_t(
                states[0], states[1], h, LANES, lam, g_ref, lam_init)[0:t].astype(o_ref.dtype)


def _diff_sample(q, cache_k, cache_v, layer, k_new, v_new, lam_params, g, lam_init):
    b, t, width = q.shape
    dv = 2 * HEAD_DIM
    heads = width // dv
    p = cache_k.shape[2]
    tk = min(ATT_TILE, p)
    assert t <= LANES and p % tk == 0
    new = pl.BlockSpec((None, t, width), lambda i, j: (i, 0, 0))
    cache = pl.BlockSpec((None, None, tk, heads, dv), lambda i, j: (layer, i, j, 0, 0))
    return pl.pallas_call(
        functools.partial(_diff_sample_kernel, lam_init=lam_init), grid=(b, p // tk),
        in_specs=[new, cache, cache, new, new] + _lambda_specs(),
        out_specs=new, out_shape=jax.ShapeDtypeStruct(q.shape, BF16),
        scratch_shapes=[pltpu.VMEM((2 * heads, LANES, HEAD_DIM), BF16)]
        + _state_scratch(heads, dv) + _state_scratch(heads, dv),
        compiler_params=_params("parallel", "arbitrary"), name="diff_sample")(
            q, cache_k, cache_v, k_new, v_new, *lam_params, g)


def _merge_kernel(h_ref, oa_ref, ob_ref, wga_ref, wgb_ref, wa_ref, wb_ref, m_ref):
    hb = h_ref[...].astype(BF16)
    oa, ob = oa_ref[...], ob_ref[...]
    for c in range(m_ref.shape[1] // COL_CHUNK):
        sl = slice(c * COL_CHUNK, (c + 1) * COL_CHUNK)
        ga = jax.nn.sigmoid(_dot(hb, wga_ref[:, sl]))
        gb = jax.nn.sigmoid(_dot(hb, wgb_ref[:, sl]))
        m_ref[:, sl] = (ga * _dot(oa, wa_ref[:, sl]) + gb * _dot(ob, wb_ref[:, sl])).astype(m_ref.dtype)


def _merge(h, oa, ob, wga, wgb, wa, wb):
    t, d = h.shape
    tm = min(ROW_TILE, t)
    rows = lambda n: pl.BlockSpec((tm, n), lambda i: (i, 0))
    return pl.pallas_call(
        _merge_kernel, grid=(pl.cdiv(t, tm),),
        in_specs=[rows(d), rows(oa.shape[1]), rows(ob.shape[1]),
                  _resident(wga.shape), _resident(wgb.shape), _resident(wa.shape), _resident(wb.shape)],
        out_specs=rows(d), out_shape=jax.ShapeDtypeStruct((t, d), BF16),
        compiler_params=_params("parallel"), name="gated_merge")(h, oa, ob, wga, wgb, wa, wb)


def _out_ln_kernel(h_ref, m_ref, w_ref, g_ref, b_ref, o_ref, *, alpha):
    y = _dot(m_ref[...], w_ref[...])
    o_ref[...] = _layer_norm(alpha * h_ref[...] + y, g_ref[...], b_ref[...])


def _out_proj_ln(h, m, w, g, b, alpha):
    t, d = h.shape
    tm = min(ROW_TILE, t)
    row = pl.BlockSpec((tm, d), lambda i: (i, 0))
    vec = pl.BlockSpec((1, d), lambda i: (0, 0))
    return pl.pallas_call(
        functools.partial(_out_ln_kernel, alpha=alpha), grid=(pl.cdiv(t, tm),),
        in_specs=[row, row, _resident(w.shape), vec, vec], out_specs=row,
        out_shape=jax.ShapeDtypeStruct((t, d), F32),
        compiler_params=_params("parallel"), name="out_proj_ln")(h, m, w, g.reshape(1, d), b.reshape(1, d))


def _mlp_kernel(x_ref, wu_ref, wd_ref, g_ref, b_ref, o_ref, xb_ref, acc_ref, *, alpha):
    f = pl.program_id(1)

    @pl.when(f == 0)
    def _():
        xb_ref[...] = x_ref[...].astype(BF16)
        acc_ref[...] = jnp.zeros(acc_ref.shape, F32)

    u = jnp.maximum(_dot(xb_ref[...], wu_ref[...]), 0.0)
    acc_ref[...] += _dot((u * u).astype(BF16), wd_ref[...])

    @pl.when(f == pl.num_programs(1) - 1)
    def _():
        o_ref[...] = _layer_norm(alpha * x_ref[...] + acc_ref[...], g_ref[...], b_ref[...])


def _mlp_ln(x, wu, wd, g, b, alpha):
    t, d = x.shape
    dff = wu.shape[1]
    tm = min(ROW_TILE, t)
    tf = min(FF_TILE, dff)
    row = pl.BlockSpec((tm, d), lambda i, f: (i, 0))
    vec = pl.BlockSpec((1, d), lambda i, f: (0, 0))
    return pl.pallas_call(
        functools.partial(_mlp_kernel, alpha=alpha), grid=(pl.cdiv(t, tm), dff // tf),
        in_specs=[row, pl.BlockSpec((d, tf), lambda i, f: (0, f)), pl.BlockSpec((tf, d), lambda i, f: (f, 0)),
                  vec, vec],
        out_specs=row, out_shape=jax.ShapeDtypeStruct((t, d), F32),
        scratch_shapes=[pltpu.VMEM((tm, d), BF16), pltpu.VMEM((tm, d), F32)],
        compiler_params=_params("parallel", "arbitrary"), name="mlp_ln")(
            x, wu, wd, g.reshape(1, d), b.reshape(1, d))


def kernel(x_prompt, x_sample, cache_fox_k, cache_fox_v, cache_fox_logf, cache_diff_k, cache_diff_v, meta_tokens, ln_in_g, ln_in_b, w_in, b_f, lambda_q1, lambda_k1, lambda_q2, lambda_k2, subln_g, w_br_a, w_br_b, w_out, ln1_g, ln1_b, w_up, w_down, ln2_g, ln2_b):
    b, s_len, d = x_prompt.shape
    bd, t_new, _ = x_sample.shape
    depth = w_in.shape[0]
    p = cache_fox_k.shape[2]
    n = N_META + s_len
    alpha = (2 * depth) ** 0.25
    fox_w = FOX_HEADS * HEAD_DIM
    diff_w = DIFF_HEADS * 2 * HEAD_DIM

    o_f = 3 * fox_w
    o_b = o_f + FOX_HEADS
    o_g = o_b + 3 * diff_w
    w_in_b = w_in.astype(BF16)
    w_fox = w_in_b[:, :, :o_f]
    w_f = jnp.pad(w_in_b[:, :, o_f:o_b], ((0, 0), (0, 0), (0, LANES - FOX_HEADS)))
    w_diff = w_in_b[:, :, o_b:o_g]
    w_ga = w_in_b[:, :, o_g:o_g + d]
    w_gb = w_in_b[:, :, o_g + d:]
    b_f_pad = jnp.pad(b_f.astype(F32), ((0, 0), (0, LANES - FOX_HEADS)))[:, None, :]
    w_a, w_b, w_o = w_br_a.astype(BF16), w_br_b.astype(BF16), w_out.astype(BF16)
    w_u, w_d = w_up.astype(BF16), w_down.astype(BF16)

    cs_f = jnp.tile(_rope_table(N_META + jnp.arange(s_len)), (b, 1))
    cs_m = jnp.tile(_rope_table(jnp.arange(N_META)), (b, 1))
    cs_s = jnp.tile(_rope_table(p + jnp.arange(t_new)), (bd, 1))

    tm_f = min(ROW_TILE, s_len)
    tm_s = min(ROW_TILE, bd * t_new)
    assert s_len % tm_f == 0 and (bd * t_new) % tm_s == 0
    blocks_per_seq = s_len // tm_f
    frame_rows = lambda i: (i // blocks_per_seq) * n + N_META + (i % blocks_per_seq) * tm_f
    meta_rows = lambda i: i * n
    sample_rows = lambda i: i * tm_s

    hf = _input_layer_norm(x_prompt.reshape(b * s_len, d), ln_in_g, ln_in_b)
    hm = jnp.tile(_input_layer_norm(meta_tokens.astype(F32), ln_in_g, ln_in_b), (b, 1))
    hs = _input_layer_norm(x_sample.reshape(bd * t_new, d), ln_in_g, ln_in_b)

    def empty_caches(rows, heads, dim):
        shape = (depth, rows * heads, dim) if dim == LANES else (depth, rows, heads, dim)
        return [jnp.zeros(shape, F32) for _ in range(2)]

    fox_p, diff_p = empty_caches(b * n, FOX_HEADS, HEAD_DIM), empty_caches(b * n, DIFF_HEADS, 2 * HEAD_DIM)
    fox_s, diff_s = empty_caches(bd * t_new, FOX_HEADS, HEAD_DIM), empty_caches(bd * t_new, DIFF_HEADS, 2 * HEAD_DIM)
    logf_p, logf_s = [], []
    for l in range(depth):
        lam_init = 0.8 - 0.6 * math.exp(-0.3 * l)
        lam_params = [a[l].reshape(1, HEAD_DIM).astype(F32) for a in (lambda_q1, lambda_k1, lambda_q2, lambda_k2)]
        g_sub = subln_g[l].reshape(1, 2 * HEAD_DIM).astype(F32)

        def project(x, cs, fox_c, diff_c, tm, row_start):
            qa, ka, va, lf, *fox_c = _project_fox(x, w_fox[l], w_f[l], b_f_pad[l], l, fox_c, tm, row_start)
            qb, kb, vb, *diff_c = _project_diff(x, w_diff[l], cs, l, diff_c, tm, row_start)
            return (qa, ka, va), lf, (qb, kb, vb), fox_c, diff_c

        def channel_mix(x, oa, ob):
            m = _merge(x, oa.reshape(-1, fox_w), ob.reshape(-1, diff_w), w_ga[l], w_gb[l], w_a[l], w_b[l])
            x = _out_proj_ln(x, m, w_o[l], ln1_g[l], ln1_b[l], alpha)
            return _mlp_ln(x, w_u[l], w_d[l], ln2_g[l], ln2_b[l], alpha)

        def per_seq(xs, batch):
            return tuple(x.reshape(batch, x.shape[0] // batch, x.shape[-1]) for x in xs)

        qkv_a, lf_f, qkv_b, fox_p, diff_p = project(hf, cs_f, fox_p, diff_p, tm_f, frame_rows)
        qkv_am, lf_m, qkv_bm, fox_p, diff_p = project(hm, cs_m, fox_p, diff_p, N_META, meta_rows)
        lf_f, lf_m = lf_f.reshape(b, s_len, FOX_HEADS), lf_m.reshape(b, N_META, FOX_HEADS)
        logf_p.append(jnp.concatenate([lf_m, lf_f], axis=1))
        a_m, a_f = _fox_bias(lf_m, lf_f)
        oa_f, oa_m = _fox_prompt(per_seq(qkv_a, b) + (a_f,), per_seq(qkv_am, b) + (a_m,))
        ob_f, ob_m = _diff_prompt(per_seq(qkv_b, b), per_seq(qkv_bm, b), lam_params, g_sub, lam_init)
        hf = channel_mix(hf, oa_f, ob_f)
        hm = channel_mix(hm, oa_m, ob_m)

        qkv_a, lf_s, qkv_b, fox_s, diff_s = project(hs, cs_s, fox_s, diff_s, tm_s, sample_rows)
        lf_s = lf_s.reshape(bd, t_new, FOX_HEADS)
        logf_s.append(lf_s)
        a_c, a_n = _fox_bias(cache_fox_logf[l].astype(F32), lf_s)
        qa, ka, va = per_seq(qkv_a, bd)
        qb, kb, vb = per_seq(qkv_b, bd)
        oa_s = _fox_sample(qa, cache_fox_k, cache_fox_v, l, a_c, ka, va, a_n)
        ob_s = _diff_sample(qb, cache_diff_k, cache_diff_v, l, kb, vb, lam_params, g_sub, lam_init)
        hs = channel_mix(hs, oa_s, ob_s)

    def per_layer_seq(caches, batch, rows, heads):
        return tuple(c.reshape(depth, batch, rows, heads, c.shape[-1]) for c in caches)

    fk_p, fv_p = per_layer_seq(fox_p, b, n, FOX_HEADS)
    dk_p, dv_p = per_layer_seq(diff_p, b, n, DIFF_HEADS)
    fk_s, fv_s = per_layer_seq(fox_s, bd, t_new, FOX_HEADS)
    dk_s, dv_s = per_layer_seq(diff_s, bd, t_new, DIFF_HEADS)
    return (hf.reshape(b, s_len, d), hs.reshape(bd, t_new, d), fk_p, fv_p, jnp.stack(logf_p), dk_p, dv_p,
            fk_s, fv_s, jnp.stack(logf_s), dk_s, dv_s)
```

```python
import functools
import math

import jax
import jax.numpy as jnp
from jax import lax
from jax.experimental import pallas as pl
from jax.experimental.pallas import tpu as pltpu

N_META = 16
CHUNK = 64
FOX_HEADS = 8
DIFF_HEADS = 4
HEAD_DIM = 128
ROT_DIM = 32
ROPE_THETA = 500000.0
LN_EPS = 1e-5
RMS_EPS = 1e-5
NEG = -1e30

LANES = 128
VMEM_LIMIT_BYTES = 56 * 2**20

ROW_TILE = 512
ATT_TILE = 512
FF_TILE = 1024
COL_CHUNK = 512
ATT_STRIP = 256
FOX_STRIPS = 16
DIFF_STRIPS = 8
ATT_AHEAD = 6
FOX_AHEAD = 8
ONES_ROWS = 16
BIAS_PARTS = 3
LOG2E = math.log2(math.e)
KEY_SCALE = HEAD_DIM ** -0.5 * LOG2E

F32 = jnp.float32
BF16 = jnp.bfloat16


def _params(*sem):
    return pltpu.CompilerParams(dimension_semantics=sem, vmem_limit_bytes=VMEM_LIMIT_BYTES)


def _resident(shape):
    return pl.BlockSpec(shape, lambda *_: (0,) * len(shape), pipeline_mode=pl.Buffered(1))


def _layer_norm(x, g, b):
    mu = jnp.mean(x, axis=-1, keepdims=True)
    xc = x - mu
    var = jnp.mean(xc * xc, axis=-1, keepdims=True)
    return xc * lax.rsqrt(var + LN_EPS) * g + b


def _dot(a, b):
    return jnp.dot(a, b, preferred_element_type=F32)


def _dot_nt(a, b):
    return lax.dot_general(a, b, (((1,), (1,)), ((), ())), preferred_element_type=F32)


def _ln_kernel(x_ref, g_ref, b_ref, o_ref):
    o_ref[...] = _layer_norm(x_ref[...], g_ref[...], b_ref[...])


def _input_layer_norm(x, g, b):
    t, d = x.shape
    tm = min(ROW_TILE, t)
    row = pl.BlockSpec((tm, d), lambda i: (i, 0))
    vec = pl.BlockSpec((1, d), lambda i: (0, 0))
    return pl.pallas_call(
        _ln_kernel, grid=(pl.cdiv(t, tm),), in_specs=[row, vec, vec], out_specs=row,
        out_shape=jax.ShapeDtypeStruct((t, d), F32), compiler_params=_params("parallel"),
        name="input_ln")(x, g.reshape(1, d), b.reshape(1, d))


def _store_cache(cache_ref, x):
    rows, dim = x.shape[0], cache_ref.shape[-1]
    heads = x.shape[1] // dim
    for head in range(heads):
        part = x[:, head * dim:(head + 1) * dim]
        if len(cache_ref.shape) == 3:
            cache_ref[0, pl.ds(head, rows, stride=heads), :] = part
        else:
            cache_ref[0, :, head, :] = part


def _cache_call(kernel_fn, name, layer, caches, heads, row_start, t, tm, in_specs, args, out_specs, out_shape):
    dim = caches[0].shape[-1]
    if caches[0].ndim == 3:
        cache_spec = pl.BlockSpec((pl.Element(1), pl.Element(tm * heads), pl.Element(dim)),
                                  lambda i: (layer, pl.multiple_of(row_start(i) * heads, heads), 0))
    else:
        cache_spec = pl.BlockSpec((pl.Element(1), pl.Element(tm), pl.Element(heads), pl.Element(dim)),
                                  lambda i: (layer, row_start(i), 0, 0))
    n_in, n_out = len(in_specs), len(out_specs)
    return pl.pallas_call(
        kernel_fn, grid=(t // tm,),
        in_specs=in_specs + [pl.BlockSpec(memory_space=pl.ANY)] * 2,
        out_specs=out_specs + [cache_spec, cache_spec],
        out_shape=out_shape + [jax.ShapeDtypeStruct(c.shape, c.dtype) for c in caches],
        input_output_aliases={n_in: n_out, n_in + 1: n_out + 1},
        compiler_params=_params("parallel"), name=name)(*args, *caches)


def _proj_fox_kernel(h_ref, w_ref, wf_ref, bf_ref, *refs, norm):
    if norm:
        g_ref, b_ref = refs[0:2]
        q_ref, k_ref, v_ref, lf_ref, hn_ref, kc_ref, vc_ref = refs[-7:]
        h = _layer_norm(h_ref[...], g_ref[...], b_ref[...])
        hn_ref[...] = h
    else:
        q_ref, k_ref, v_ref, lf_ref, kc_ref, vc_ref = refs[-6:]
        h = h_ref[...]
    hb = h.astype(BF16)
    w = q_ref.shape[1]
    q_ref[...] = _dot(hb, w_ref[:, 0:w]).astype(BF16)
    for part, (ref, cache_ref, operand_scale) in enumerate(((k_ref, kc_ref, KEY_SCALE), (v_ref, vc_ref, 1.0)), start=1):
        z = _dot(hb, w_ref[:, part * w:(part + 1) * w])
        ref[...] = (z if operand_scale == 1.0 else z * operand_scale).astype(BF16)
        _store_cache(cache_ref, z)
    fa = _dot(hb, wf_ref[...]) + bf_ref[...]
    lf = jnp.minimum(fa, 0.0) - jnp.log1p(jnp.exp(-jnp.abs(fa)))
    lf_ref[...] = lf[:, :lf_ref.shape[1]]


def _project_fox(h, w, wf, bf, layer, caches, tm, row_start, norm=None):
    t, d = h.shape
    width = w.shape[1] // 3
    rows = lambda n: pl.BlockSpec((tm, n), lambda i: (i, 0))
    vec = pl.BlockSpec((1, d), lambda i: (0, 0))
    act = jax.ShapeDtypeStruct((t, width), BF16)
    extra_in = [] if norm is None else [vec, vec]
    extra_args = () if norm is None else tuple(a.reshape(1, d).astype(F32) for a in norm)
    extra_out = [] if norm is None else [(rows(d), jax.ShapeDtypeStruct((t, d), F32))]
    return _cache_call(
        functools.partial(_proj_fox_kernel, norm=norm is not None), "proj_fox", layer, caches, FOX_HEADS,
        row_start, t, tm,
        [rows(d), _resident(w.shape), _resident(wf.shape), _resident(bf.shape)] + extra_in,
        (h, w, wf, bf) + extra_args,
        [rows(width), rows(width), rows(width), rows(FOX_HEADS)] + [s for s, _ in extra_out],
        [act, act, act, jax.ShapeDtypeStruct((t, FOX_HEADS), F32)] + [a for _, a in extra_out])


def _proj_diff_kernel(h_ref, w_ref, cs_ref, *refs):
    q_ref, k_ref, v_ref, kc_ref, vc_ref = refs[-5:]
    hb = h_ref[...].astype(BF16)
    tm, w = q_ref.shape
    cos = cs_ref[:, 0:LANES]
    sin = cs_ref[:, LANES:2 * LANES]
    half = ROT_DIM // 2
    first = lax.broadcasted_iota(jnp.int32, (tm, LANES), 1) < half

    def rope(x):
        partner = jnp.where(first, pltpu.roll(x, LANES - half, 1), pltpu.roll(x, half, 1))
        return x * cos + partner * sin

    def rotated(z):
        return jnp.concatenate([rope(z[:, g * LANES:(g + 1) * LANES]) for g in range(w // LANES)], axis=1)

    q_ref[...] = rotated(_dot(hb, w_ref[:, 0:w])).astype(BF16)
    k = rotated(_dot(hb, w_ref[:, w:2 * w]))
    k_ref[...] = (k * KEY_SCALE).astype(BF16)
    _store_cache(kc_ref, k)
    v = _dot(hb, w_ref[:, 2 * w:3 * w])
    v_ref[...] = v.astype(BF16)
    _store_cache(vc_ref, v)


def _project_diff(h, w, cs, layer, caches, tm, row_start):
    t, d = h.shape
    width = w.shape[1] // 3
    rows = lambda n: pl.BlockSpec((tm, n), lambda i: (i, 0))
    act = jax.ShapeDtypeStruct((t, width), BF16)
    return _cache_call(
        _proj_diff_kernel, "proj_diff", layer, caches, DIFF_HEADS, row_start, t, tm,
        [rows(d), _resident(w.shape), rows(2 * LANES)], (h, w, cs),
        [rows(width), rows(width), rows(width)], [act, act, act])


def _rope_table(pos):
    half = ROT_DIM // 2
    inv_freq = ROPE_THETA ** (-jnp.arange(half, dtype=F32) / half)
    ang = pos.astype(F32)[:, None] * inv_freq[None, :]
    cos, sin = jnp.cos(ang), jnp.sin(ang)
    n = pos.shape[0]
    ones = jnp.ones((n, LANES - ROT_DIM), F32)
    zeros = jnp.zeros((n, LANES - ROT_DIM), F32)
    return jnp.concatenate([cos, cos, ones, -sin, sin, zeros], axis=1)


def _split_bf16(x, parts):
    out, rest = [], x
    for _ in range(parts):
        t = rest.astype(BF16)
        out.append(t)
        rest = rest - t.astype(F32)
    return out


def _fox_bias_kernel(lf0_ref, lf1_ref, a0_ref, a1_ref):
    row = lax.broadcasted_iota(jnp.int32, (LANES, LANES), 0)
    col = lax.broadcasted_iota(jnp.int32, (LANES, LANES), 1)
    tri = jnp.where(col <= row, 1.0, 0.0).astype(BF16)
    spread = [jnp.where(col == BIAS_PARTS * row + t, 1.0, 0.0).astype(BF16) for t in range(BIAS_PARTS)]
    carry = jnp.zeros((1, LANES), F32)
    for lf_ref, a_ref in ((lf0_ref, a0_ref), (lf1_ref, a1_ref)):
        n = lf_ref.shape[0]
        for r0 in range(0, n, LANES):
            rows = min(LANES, n - r0)
            x = lf_ref[r0:r0 + rows]
            cs = carry
            for term in _split_bf16(x, BIAS_PARTS):
                cs = cs + _dot(tri[:rows, :rows], term)
            carry = cs[rows - 1:rows]
            a = cs * (-LOG2E)
            out = jnp.zeros((rows, LANES), F32)
            for term, e in zip(_split_bf16(a, BIAS_PARTS), spread):
                out = out + _dot(term, e)
            a_ref[r0:r0 + rows] = out.astype(BF16)


def _fox_bias(logf0, logf1):
    b, _, h = logf0.shape
    pad = lambda x: jnp.pad(x, ((0, 0), (0, 0), (0, LANES - h)))
    spec = lambda x: pl.BlockSpec((None, x.shape[1], LANES), lambda i: (i, 0, 0))
    return pl.pallas_call(
        _fox_bias_kernel, grid=(b,),
        in_specs=[spec(logf0), spec(logf1)], out_specs=[spec(logf0), spec(logf1)],
        out_shape=[jax.ShapeDtypeStruct((b, x.shape[1], LANES), BF16) for x in (logf0, logf1)],
        compiler_params=_params("parallel"), name="fox_bias")(pad(logf0), pad(logf1))


def _attend(stages, ahead=ATT_AHEAD):
    pending = {}
    for t in range(min(ahead, len(stages))):
        pending[t] = stages[t][0]()
    for t, (_, mask, (m_ref, acc_ref), slot, w, vt_blk) in enumerate(stages):
        if t + ahead < len(stages):
            pending[t + ahead] = stages[t + ahead][0]()
        st = pending.pop(t)
        if mask is not None:
            st = jnp.where(mask, st, NEG)
        m_prev = m_ref[slot, :, 0:w]
        m_new = jnp.maximum(m_prev, jnp.max(st, axis=0, keepdims=True))
        alpha = jnp.exp2(m_prev - m_new)
        p = jnp.exp2(st - m_new)
        m_ref[slot, :, 0:w] = m_new
        vt = vt_blk() if callable(vt_blk) else vt_blk
        acc_ref[slot, :, 0:w] = alpha * acc_ref[slot, :, 0:w] + _dot(vt, p.astype(BF16))


def _reset(state, slot, w):
    m_ref, acc_ref = state
    m_ref[slot, :, 0:w] = jnp.full((1, w), NEG, F32)
    acc_ref[slot, :, 0:w] = jnp.zeros((acc_ref.shape[1], w), F32)


def _normalised(state, slot, w):
    m_ref, acc_ref = state
    dv = acc_ref.shape[1] - ONES_ROWS
    return acc_ref[slot, 0:dv, 0:w] * (1.0 / acc_ref[slot, dv:dv + 1, 0:w])


def _state_scratch(slots, dv):
    return [pltpu.VMEM((slots, 1, ATT_STRIP), F32), pltpu.VMEM((slots, dv + ONES_ROWS, ATT_STRIP), F32)]


def _pad_rows(x, rows):
    return jnp.concatenate([x, jnp.zeros((rows - x.shape[0], x.shape[1]), x.dtype)], axis=0)


def _transpose_values(v_ref, vm_ref, vt, vtm):
    s_len, dv = v_ref.shape
    sw = ATT_STRIP
    vt[:, dv:, :] = jnp.ones((vt.shape[0], ONES_ROWS, sw), BF16)
    vtm[dv:, :] = jnp.ones((ONES_ROWS, LANES), BF16)
    for j in range(s_len // sw):
        for c in range(sw // LANES):
            r0 = j * sw + c * LANES
            for e in range(dv // LANES):
                vt[j, e * LANES:(e + 1) * LANES, c * LANES:(c + 1) * LANES] = (
                    v_ref[r0:r0 + LANES, e * LANES:(e + 1) * LANES].T)
    vmeta = _pad_rows(vm_ref[...], LANES)
    for e in range(dv // LANES):
        vtm[e * LANES:(e + 1) * LANES] = vmeta[:, e * LANES:(e + 1) * LANES].T


def _fox_prompt_kernel(q_ref, k_ref, v_ref, a_ref, qm_ref, km_ref, vm_ref, am_ref, o_ref, om_ref,
                       qa, ka, vt, vtm, m_ref, acc_ref, *, ns):
    s_len = q_ref.shape[0]
    sw = ATT_STRIP
    tq = ns * sw
    d = HEAD_DIM
    state = (m_ref, acc_ref)
    lane = lax.broadcasted_iota(jnp.int32, (1, LANES), 1)
    first = pl.program_id(1) * BIAS_PARTS
    onehot = jnp.where((lane >= first) & (lane < first + BIAS_PARTS), 1.0, 0.0).astype(BF16)
    qa[0:s_len, 0:d] = q_ref[...]
    qa[s_len:, 0:d] = _pad_rows(qm_ref[...], LANES)
    qa[:, d:] = jnp.broadcast_to(onehot, (s_len + LANES, LANES))
    ka[0:s_len, 0:d] = k_ref[...]
    ka[0:s_len, d:] = a_ref[...]
    ka[s_len:, 0:d] = _pad_rows(km_ref[...], LANES)
    ka[s_len:, d:] = _pad_rows(am_ref[...], LANES)
    _transpose_values(v_ref, vm_ref, vt, vtm)
    k_meta = ka[s_len:s_len + LANES]

    kidx = lax.broadcasted_iota(jnp.int32, (LANES, LANES), 0)
    qidx = lax.broadcasted_iota(jnp.int32, (LANES, LANES), 1)
    _reset(state, 0, LANES)
    _attend([(lambda: _dot_nt(k_meta, qa[s_len:s_len + LANES]), (kidx <= qidx) & (kidx < N_META), state, 0, LANES,
              vtm[...])])
    om_ref[...] = _normalised(state, 0, LANES).T[0:N_META].astype(om_ref.dtype)

    meta_valid = lax.broadcasted_iota(jnp.int32, (LANES, sw), 0) < N_META
    diag = lax.broadcasted_iota(jnp.int32, (sw, sw), 0) <= lax.broadcasted_iota(jnp.int32, (sw, sw), 1)

    def q_block(i, carry):
        whole = isinstance(i, int)
        r0 = i * tq if whole else pl.multiple_of(i * tq, tq)

        def scores(k_blk, c):
            return lambda: _dot_nt(k_blk, qa[pl.ds(r0 + c * sw, sw)])

        for c in range(ns):
            _reset(state, c, sw)
        stages = [(scores(k_meta, c), meta_valid, state, c, sw, vtm[...]) for c in range(ns)]
        if not whole:
            _attend(stages, FOX_AHEAD)

            def kv_block(j, cc):
                k_blk = ka[pl.ds(pl.multiple_of(j * sw, sw), sw)]
                _attend([(scores(k_blk, c), None, state, c, sw, vt[j]) for c in range(ns)], FOX_AHEAD)
                return cc

            lax.fori_loop(0, i * ns, kv_block, 0)
            stages = []
        for dd in range(ns):
            k_blk = ka[pl.ds(r0 + dd * sw, sw)]
            stages += [(scores(k_blk, c), diag if c == dd else None, state, c, sw, vt[i * ns + dd])
                       for c in range(dd, ns)]
        _attend(stages, FOX_AHEAD)
        for c in range(ns):
            o_ref[pl.ds(r0 + c * sw, sw)] = _normalised(state, c, sw).T.astype(o_ref.dtype)
        return carry

    if s_len == tq:
        q_block(0, 0)
    else:
        lax.fori_loop(0, s_len // tq, q_block, 0)


def _fox_prompt(frames, meta):
    b, s_len, width = frames[0].shape
    d = HEAD_DIM
    ns = min(FOX_STRIPS, s_len // ATT_STRIP)
    head = lambda rows: pl.BlockSpec((None, rows, d), lambda i, h: (i, 0, h))
    bias = lambda rows: pl.BlockSpec((None, rows, LANES), lambda i, h: (i, 0, 0))
    group = lambda rows: [head(rows)] * 3 + [bias(rows)]
    return pl.pallas_call(
        functools.partial(_fox_prompt_kernel, ns=ns), grid=(b, width // d),
        in_specs=group(s_len) + group(N_META), out_specs=[head(s_len), head(N_META)],
        out_shape=[jax.ShapeDtypeStruct(frames[0].shape, BF16), jax.ShapeDtypeStruct(meta[0].shape, BF16)],
        scratch_shapes=[pltpu.VMEM((s_len + LANES, 2 * d), BF16), pltpu.VMEM((s_len + LANES, 2 * d), BF16),
                        pltpu.VMEM((s_len // ATT_STRIP, d + ONES_ROWS, ATT_STRIP), BF16),
                        pltpu.VMEM((d + ONES_ROWS, LANES), BF16)]
        + _state_scratch(ns, d),
        compiler_params=_params("parallel", "parallel"), name="fox_prompt")(*frames, *meta)


def _values_t(v):
    keys, dv = v.shape
    cols = [jnp.concatenate([v[c * LANES:(c + 1) * LANES, e * LANES:(e + 1) * LANES].T
                             for e in range(dv // LANES)], axis=0) for c in range(keys // LANES)]
    return jnp.concatenate([jnp.concatenate(cols, axis=1).astype(BF16), jnp.ones((ONES_ROWS, keys), BF16)], axis=0)


def _fox_sample_kernel(q_ref, kc_ref, vc_ref, ac_ref, kn_ref, vn_ref, an_ref, o_ref, qa, m_ref, acc_ref):
    j = pl.program_id(1)
    t = q_ref.shape[0]
    d = HEAD_DIM
    heads = qa.shape[0]
    tk = ac_ref.shape[0]
    state = (m_ref, acc_ref)

    def head_rows(ref, h):
        return ref[pl.ds(h, tk, stride=heads), :]

    @pl.when(j == 0)
    def _():
        lane = lax.broadcasted_iota(jnp.int32, (1, LANES), 1)
        for h in range(heads):
            onehot = jnp.where((lane >= h * BIAS_PARTS) & (lane < (h + 1) * BIAS_PARTS), 1.0, 0.0).astype(BF16)
            qa[h, :, 0:d] = _pad_rows(q_ref[:, h * d:(h + 1) * d], LANES)
            qa[h, :, d:] = jnp.broadcast_to(onehot, (LANES, LANES))
            _reset(state, h, LANES)

    a_blk = ac_ref[...]
    _attend([(functools.partial(
        lambda h: _dot_nt(jnp.concatenate([(head_rows(kc_ref, h) * KEY_SCALE).astype(BF16), a_blk], axis=1),
                          qa[h]), h),
              None, state, h, LANES, functools.partial(lambda h: _values_t(head_rows(vc_ref, h)), h))
             for h in range(heads)])

    @pl.when(j == pl.num_programs(1) - 1)
    def _():
        causal = (lax.broadcasted_iota(jnp.int32, (LANES, LANES), 0)
                  <= lax.broadcasted_iota(jnp.int32, (LANES, LANES), 1))
        a_new = an_ref[...]
        _attend([(functools.partial(
            lambda h: _dot_nt(_pad_rows(jnp.concatenate([kn_ref[:, h * d:(h + 1) * d], a_new], axis=1), LANES),
                              qa[h]), h),
                  causal, state, h, LANES,
                  functools.partial(lambda h: _values_t(_pad_rows(vn_ref[:, h * d:(h + 1) * d], LANES)), h))
                 for h in range(heads)])
        for h in range(heads):
            o_ref[:, h * d:(h + 1) * d] = _normalised(state, h, LANES).T[0:t].astype(o_ref.dtype)


def _fox_sample(q, cache_k, cache_v, layer, a_cache, k_new, v_new, a_new):
    b, t, width = q.shape
    d = HEAD_DIM
    heads = width // d
    p = cache_k.shape[2]
    tk = min(ATT_TILE, p)
    assert t <= LANES and p % tk == 0
    new = lambda n: pl.BlockSpec((None, t, n), lambda i, j: (i, 0, 0))
    flat = lambda c: c.reshape(c.shape[0], b, p * heads, d)
    cache = pl.BlockSpec((None, None, tk * heads, d), lambda i, j: (layer, i, j, 0))
    return pl.pallas_call(
        _fox_sample_kernel, grid=(b, p // tk),
        in_specs=[new(width), cache, cache, pl.BlockSpec((None, tk, LANES), lambda i, j: (i, j, 0)),
                  new(width), new(width), new(LANES)],
        out_specs=new(width), out_shape=jax.ShapeDtypeStruct(q.shape, BF16),
        scratch_shapes=[pltpu.VMEM((heads, LANES, 2 * d), BF16)] + _state_scratch(heads, d),
        compiler_params=_params("parallel", "arbitrary"), name="fox_sample")(
            q, flat(cache_k), flat(cache_v), a_cache, k_new, v_new, a_new)


def _diff_lambda(lq1_ref, lk1_ref, lq2_ref, lk2_ref, lam_init):
    d1 = jnp.sum(lq1_ref[...] * lk1_ref[...], axis=1, keepdims=True)
    d2 = jnp.sum(lq2_ref[...] * lk2_ref[...], axis=1, keepdims=True)
    return jnp.exp(d1) - jnp.exp(d2) + lam_init


def _diff_finish_t(s1, s2, slot, w, lam, g_ref, lam_init):
    ot = _normalised(s1, slot, w) - lam * _normalised(s2, slot, w)
    ot = ot * lax.rsqrt(jnp.mean(ot * ot, axis=0, keepdims=True) + RMS_EPS)
    return ot.T * g_ref[...] * (1.0 - lam_init)


def _diff_prompt_kernel(q_ref, k_ref, v_ref, qm_ref, km_ref, vm_ref, lq1_ref, lk1_ref, lq2_ref, lk2_ref, g_ref,
                        o_ref, om_ref, vt, vtm, m1, a1, m2, a2, *, ns, lam_init):
    s_len = q_ref.shape[0]
    sw = ATT_STRIP
    tq = ns * sw
    d = HEAD_DIM
    states = ((m1, a1), (m2, a2))
    lam = _diff_lambda(lq1_ref, lk1_ref, lq2_ref, lk2_ref, lam_init)
    _transpose_values(v_ref, vm_ref, vt, vtm)

    def both(keys, queries, c, w, mask, vt_blk):
        return [(functools.partial(lambda lo: _dot_nt(keys(lo), queries(lo)), e * d), mask, states[e], c, w, vt_blk)
                for e in range(2)]

    def meta_keys(lo):
        return _pad_rows(km_ref[:, lo:lo + d], LANES)

    def frame_rows(ref, rows):
        return lambda lo: ref[rows, lo:lo + d]

    for st in states:
        _reset(st, 0, LANES)
    real_keys = lax.broadcasted_iota(jnp.int32, (LANES, LANES), 0) < N_META
    _attend(both(meta_keys, lambda lo: _pad_rows(qm_ref[:, lo:lo + d], LANES), 0, LANES, real_keys, vtm[...]))
    om_ref[...] = _diff_finish_t(states[0], states[1], 0, LANES, lam, g_ref, lam_init)[0:N_META].astype(om_ref.dtype)

    meta_valid = lax.broadcasted_iota(jnp.int32, (LANES, sw), 0) < N_META
    diag = (lax.broadcasted_iota(jnp.int32, (sw, sw), 0) // CHUNK
            <= lax.broadcasted_iota(jnp.int32, (sw, sw), 1) // CHUNK)

    def q_block(i, carry):
        r0 = pl.multiple_of(i * tq, tq)

        def q_rows(c):
            return pl.ds(r0 + c * sw, sw)

        for c in range(ns):
            for st in states:
                _reset(st, c, sw)
        stages = []
        for c in range(ns):
            stages += both(meta_keys, frame_rows(q_ref, q_rows(c)), c, sw, meta_valid, vtm[...])
        _attend(stages)

        def kv_block(j, cc):
            keys = frame_rows(k_ref, pl.ds(pl.multiple_of(j * sw, sw), sw))
            stages = []
            for c in range(ns):
                stages += both(keys, frame_rows(q_ref, q_rows(c)), c, sw, None, vt[j])
            _attend(stages)
            return cc

        lax.fori_loop(0, i * ns, kv_block, 0)
        stages = []
        for dd in range(ns):
            keys = frame_rows(k_ref, pl.ds(r0 + dd * sw, sw))
            for c in range(dd, ns):
                stages += both(keys, frame_rows(q_ref, q_rows(c)), c, sw, diag if c == dd else None, vt[i * ns + dd])
        _attend(stages)
        for c in range(ns):
            o_ref[q_rows(c)] = _diff_finish_t(states[0], states[1], c, sw, lam, g_ref, lam_init).astype(o_ref.dtype)
        return carry

    lax.fori_loop(0, s_len // tq, q_block, 0)


def _lambda_specs():
    return [pl.BlockSpec((1, HEAD_DIM), lambda i, h: (0, 0))] * 4 + [
        pl.BlockSpec((1, 2 * HEAD_DIM), lambda i, h: (0, 0))]


def _diff_prompt(frames, meta, lam_params, g, lam_init):
    b, s_len, width = frames[0].shape
    dv = 2 * HEAD_DIM
    ns = min(DIFF_STRIPS, s_len // ATT_STRIP)
    head = lambda rows: pl.BlockSpec((None, rows, dv), lambda i, h: (i, 0, h))
    return pl.pallas_call(
        functools.partial(_diff_prompt_kernel, ns=ns, lam_init=lam_init), grid=(b, width // dv),
        in_specs=[head(s_len)] * 3 + [head(N_META)] * 3 + _lambda_specs(),
        out_specs=[head(s_len), head(N_META)],
        out_shape=[jax.ShapeDtypeStruct(frames[0].shape, BF16), jax.ShapeDtypeStruct(meta[0].shape, BF16)],
        scratch_shapes=[pltpu.VMEM((s_len // ATT_STRIP, dv + ONES_ROWS, ATT_STRIP), BF16),
                        pltpu.VMEM((dv + ONES_ROWS, LANES), BF16)]
        + _state_scratch(ns, dv) + _state_scratch(ns, dv),
        compiler_params=_params("parallel", "parallel"), name="diff_prompt")(*frames, *meta, *lam_params, g)


def _diff_sample_kernel(q_ref, kc_ref, vc_ref, kn_ref, vn_ref, lq1_ref, lk1_ref, lq2_ref, lk2_ref, g_ref, o_ref,
                        qp, m1, a1, m2, a2, *, lam_init):
    j = pl.program_id(1)
    t = q_ref.shape[0]
    d = HEAD_DIM
    heads = kc_ref.shape[1]
    states = ((m1, a1), (m2, a2))

    @pl.when(j == 0)
    def _():
        for h in range(heads):
            for e in range(2):
                qp[2 * h + e] = _pad_rows(q_ref[:, (2 * h + e) * d:(2 * h + e + 1) * d], LANES)
                _reset(states[e], h, LANES)

    def stages(keys, values, mask):
        return [(functools.partial(lambda h, e: _dot_nt(keys(h, e), qp[2 * h + e]), h, e), mask, states[e], h, LANES,
                 functools.partial(lambda h: _values_t(values(h)), h)) for h in range(heads) for e in range(2)]

    _attend(stages(lambda h, e: (kc_ref[:, h, e * d:(e + 1) * d] * KEY_SCALE).astype(BF16),
                   lambda h: vc_ref[:, h, :], None))

    @pl.when(j == pl.num_programs(1) - 1)
    def _():
        lam = _diff_lambda(lq1_ref, lk1_ref, lq2_ref, lk2_ref, lam_init)
        real = lax.broadcasted_iota(jnp.int32, (LANES, LANES), 0) < t
        _attend(stages(lambda h, e: _pad_rows(kn_ref[:, (2 * h + e) * d:(2 * h + e + 1) * d], LANES),
                       lambda h: _pad_rows(vn_ref[:, 2 * h * d:2 * (h + 1) * d], LANES), real))
        for h in range(heads):
            o_ref[:, 2 * h * d:2 * (h + 1) * d] = _diff_finish_t(
                states[0], states[1], h, LANES, lam, g_ref, lam_init)[0:t].astype(o_ref.dtype)


def _diff_sample(q, cache_k, cache_v, layer, k_new, v_new, lam_params, g, lam_init):
    b, t, width = q.shape
    dv = 2 * HEAD_DIM
    heads = width // dv
    p = cache_k.shape[2]
    tk = min(ATT_TILE, p)
    assert t <= LANES and p % tk == 0
    new = pl.BlockSpec((None, t, width), lambda i, j: (i, 0, 0))
    cache = pl.BlockSpec((None, None, tk, heads, dv), lambda i, j: (layer, i, j, 0, 0))
    return pl.pallas_call(
        functools.partial(_diff_sample_kernel, lam_init=lam_init), grid=(b, p // tk),
        in_specs=[new, cache, cache, new, new] + _lambda_specs(),
        out_specs=new, out_shape=jax.ShapeDtypeStruct(q.shape, BF16),
        scratch_shapes=[pltpu.VMEM((2 * heads, LANES, HEAD_DIM), BF16)]
        + _state_scratch(heads, dv) + _state_scratch(heads, dv),
        compiler_params=_params("parallel", "arbitrary"), name="diff_sample")(
            q, cache_k, cache_v, k_new, v_new, *lam_params, g)


def _merge_kernel(h_ref, oa_ref, ob_ref, wga_ref, wgb_ref, wa_ref, wb_ref, m_ref):
    hb = h_ref[...].astype(BF16)
    oa, ob = oa_ref[...], ob_ref[...]
    for c in range(m_ref.shape[1] // COL_CHUNK):
        sl = slice(c * COL_CHUNK, (c + 1) * COL_CHUNK)
        ga = jax.nn.sigmoid(_dot(hb, wga_ref[:, sl]))
        gb = jax.nn.sigmoid(_dot(hb, wgb_ref[:, sl]))
        m_ref[:, sl] = (ga * _dot(oa, wa_ref[:, sl]) + gb * _dot(ob, wb_ref[:, sl])).astype(m_ref.dtype)


def _merge(h, oa, ob, wga, wgb, wa, wb):
    t, d = h.shape
    tm = min(ROW_TILE, t)
    rows = lambda n: pl.BlockSpec((tm, n), lambda i: (i, 0))
    return pl.pallas_call(
        _merge_kernel, grid=(pl.cdiv(t, tm),),
        in_specs=[rows(d), rows(oa.shape[1]), rows(ob.shape[1]),
                  _resident(wga.shape), _resident(wgb.shape), _resident(wa.shape), _resident(wb.shape)],
        out_specs=rows(d), out_shape=jax.ShapeDtypeStruct((t, d), BF16),
        compiler_params=_params("parallel"), name="gated_merge")(h, oa, ob, wga, wgb, wa, wb)


def _out_ln_kernel(h_ref, m_ref, w_ref, g_ref, b_ref, o_ref, *, alpha):
    y = _dot(m_ref[...], w_ref[...])
    o_ref[...] = _layer_norm(alpha * h_ref[...] + y, g_ref[...], b_ref[...])


def _out_proj_ln(h, m, w, g, b, alpha):
    t, d = h.shape
    tm = min(ROW_TILE, t)
    row = pl.BlockSpec((tm, d), lambda i: (i, 0))
    vec = pl.BlockSpec((1, d), lambda i: (0, 0))
    return pl.pallas_call(
        functools.partial(_out_ln_kernel, alpha=alpha), grid=(pl.cdiv(t, tm),),
        in_specs=[row, row, _resident(w.shape), vec, vec], out_specs=row,
        out_shape=jax.ShapeDtypeStruct((t, d), F32),
        compiler_params=_params("parallel"), name="out_proj_ln")(h, m, w, g.reshape(1, d), b.reshape(1, d))


def _mlp_kernel(x_ref, wu_ref, wd_ref, g_ref, b_ref, o_ref, xb_ref, acc_ref, *, alpha):
    f = pl.program_id(1)

    @pl.when(f == 0)
    def _():
        xb_ref[...] = x_ref[...].astype(BF16)
        acc_ref[...] = jnp.zeros(acc_ref.shape, F32)

    u = jnp.maximum(_dot(xb_ref[...], wu_ref[...]), 0.0)
    acc_ref[...] += _dot((u * u).astype(BF16), wd_ref[...])

    @pl.when(f == pl.num_programs(1) - 1)
    def _():
        o_ref[...] = _layer_norm(alpha * x_ref[...] + acc_ref[...], g_ref[...], b_ref[...])


def _mlp_ln(x, wu, wd, g, b, alpha):
    t, d = x.shape
    dff = wu.shape[1]
    tm = min(ROW_TILE, t)
    tf = min(FF_TILE, dff)
    row = pl.BlockSpec((tm, d), lambda i, f: (i, 0))
    vec = pl.BlockSpec((1, d), lambda i, f: (0, 0))
    return pl.pallas_call(
        functools.partial(_mlp_kernel, alpha=alpha), grid=(pl.cdiv(t, tm), dff // tf),
        in_specs=[row, pl.BlockSpec((d, tf), lambda i, f: (0, f)), pl.BlockSpec((tf, d), lambda i, f: (f, 0)),
                  vec, vec],
        out_specs=row, out_shape=jax.ShapeDtypeStruct((t, d), F32),
        scratch_shapes=[pltpu.VMEM((tm, d), BF16), pltpu.VMEM((tm, d), F32)],
        compiler_params=_params("parallel", "arbitrary"), name="mlp_ln")(
            x, wu, wd, g.reshape(1, d), b.reshape(1, d))


def kernel(x_prompt, x_sample, cache_fox_k, cache_fox_v, cache_fox_logf, cache_diff_k, cache_diff_v, meta_tokens, ln_in_g, ln_in_b, w_in, b_f, lambda_q1, lambda_k1, lambda_q2, lambda_k2, subln_g, w_br_a, w_br_b, w_out, ln1_g, ln1_b, w_up, w_down, ln2_g, ln2_b):
    b, s_len, d = x_prompt.shape
    bd, t_new, _ = x_sample.shape
    depth = w_in.shape[0]
    p = cache_fox_k.shape[2]
    n = N_META + s_len
    alpha = (2 * depth) ** 0.25
    fox_w = FOX_HEADS * HEAD_DIM
    diff_w = DIFF_HEADS * 2 * HEAD_DIM

    o_f = 3 * fox_w
    o_b = o_f + FOX_HEADS
    o_g = o_b + 3 * diff_w
    w_in_b = w_in.astype(BF16)
    w_fox = w_in_b[:, :, :o_f]
    w_f = jnp.pad(w_in_b[:, :, o_f:o_b], ((0, 0), (0, 0), (0, LANES - FOX_HEADS)))
    w_diff = w_in_b[:, :, o_b:o_g]
    w_ga = w_in_b[:, :, o_g:o_g + d]
    w_gb = w_in_b[:, :, o_g + d:]
    b_f_pad = jnp.pad(b_f.astype(F32), ((0, 0), (0, LANES - FOX_HEADS)))[:, None, :]
    w_a, w_b, w_o = w_br_a.astype(BF16), w_br_b.astype(BF16), w_out.astype(BF16)
    w_u, w_d = w_up.astype(BF16), w_down.astype(BF16)

    cs_f = jnp.tile(_rope_table(N_META + jnp.arange(s_len)), (b, 1))
    cs_m = jnp.tile(_rope_table(jnp.arange(N_META)), (b, 1))
    cs_s = jnp.tile(_rope_table(p + jnp.arange(t_new)), (bd, 1))

    tm_f = min(ROW_TILE, s_len)
    tm_s = min(ROW_TILE, bd * t_new)
    assert s_len % tm_f == 0 and (bd * t_new) % tm_s == 0
    blocks_per_seq = s_len // tm_f
    frame_rows = lambda i: (i // blocks_per_seq) * n + N_META + (i % blocks_per_seq) * tm_f
    meta_rows = lambda i: i * n
    sample_rows = lambda i: i * tm_s

    hf = x_prompt.reshape(b * s_len, d).astype(F32)
    hm = jnp.tile(_input_layer_norm(meta_tokens.astype(F32), ln_in_g, ln_in_b), (b, 1))
    hs = _input_layer_norm(x_sample.reshape(bd * t_new, d), ln_in_g, ln_in_b)

    def empty_caches(rows, heads, dim):
        shape = (depth, rows * heads, dim) if dim == LANES else (depth, rows, heads, dim)
        return [jnp.zeros(shape, F32) for _ in range(2)]

    fox_p, diff_p = empty_caches(b * n, FOX_HEADS, HEAD_DIM), empty_caches(b * n, DIFF_HEADS, 2 * HEAD_DIM)
    fox_s, diff_s = empty_caches(bd * t_new, FOX_HEADS, HEAD_DIM), empty_caches(bd * t_new, DIFF_HEADS, 2 * HEAD_DIM)
    logf_p, logf_s = [], []
    for l in range(depth):
        lam_init = 0.8 - 0.6 * math.exp(-0.3 * l)
        lam_params = [a[l].reshape(1, HEAD_DIM).astype(F32) for a in (lambda_q1, lambda_k1, lambda_q2, lambda_k2)]
        g_sub = subln_g[l].reshape(1, 2 * HEAD_DIM).astype(F32)

        def project(x, cs, fox_c, diff_c, tm, row_start, norm=None):
            qa, ka, va, lf, *rest = _project_fox(x, w_fox[l], w_f[l], b_f_pad[l], l, fox_c, tm, row_start, norm)
            if norm is not None:
                x, *rest = rest
            qb, kb, vb, *diff_c = _project_diff(x, w_diff[l], cs, l, diff_c, tm, row_start)
            return (qa, ka, va), lf, (qb, kb, vb), rest, diff_c, x

        def channel_mix(x, oa, ob):
            m = _merge(x, oa.reshape(-1, fox_w), ob.reshape(-1, diff_w), w_ga[l], w_gb[l], w_a[l], w_b[l])
            x = _out_proj_ln(x, m, w_o[l], ln1_g[l], ln1_b[l], alpha)
            return _mlp_ln(x, w_u[l], w_d[l], ln2_g[l], ln2_b[l], alpha)

        def per_seq(xs, batch):
            return tuple(x.reshape(batch, x.shape[0] // batch, x.shape[-1]) for x in xs)

        qkv_a, lf_f, qkv_b, fox_p, diff_p, hf = project(hf, cs_f, fox_p, diff_p, tm_f, frame_rows,
                                                        (ln_in_g, ln_in_b) if l == 0 else None)
        qkv_am, lf_m, qkv_bm, fox_p, diff_p, _ = project(hm, cs_m, fox_p, diff_p, N_META, meta_rows)
        lf_f, lf_m = lf_f.reshape(b, s_len, FOX_HEADS), lf_m.reshape(b, N_META, FOX_HEADS)
        logf_p.append(jnp.concatenate([lf_m, lf_f], axis=1))
        a_m, a_f = _fox_bias(lf_m, lf_f)
        oa_f, oa_m = _fox_prompt(per_seq(qkv_a, b) + (a_f,), per_seq(qkv_am, b) + (a_m,))
        ob_f, ob_m = _diff_prompt(per_seq(qkv_b, b), per_seq(qkv_bm, b), lam_params, g_sub, lam_init)
        hf = channel_mix(hf, oa_f, ob_f)
        hm = channel_mix(hm, oa_m, ob_m)

        qkv_a, lf_s, qkv_b, fox_s, diff_s, _ = project(hs, cs_s, fox_s, diff_s, tm_s, sample_rows)
        lf_s = lf_s.reshape(bd, t_new, FOX_HEADS)
        logf_s.append(lf_s)
        a_c, a_n = _fox_bias(cache_fox_logf[l].astype(F32), lf_s)
        qa, ka, va = per_seq(qkv_a, bd)
        qb, kb, vb = per_seq(qkv_b, bd)
        oa_s = _fox_sample(qa, cache_fox_k, cache_fox_v, l, a_c, ka, va, a_n)
        ob_s = _diff_sample(qb, cache_diff_k, cache_diff_v, l, kb, vb, lam_params, g_sub, lam_init)
        hs = channel_mix(hs, oa_s, ob_s)

    def per_layer_seq(caches, batch, rows, heads):
        return tuple(c.reshape(depth, batch, rows, heads, c.shape[-1]) for c in caches)

    fk_p, fv_p = per_layer_seq(fox_p, b, n, FOX_HEADS)
    dk_p, dv_p = per_layer_seq(diff_p, b, n, DIFF_HEADS)
    fk_s, fv_s = per_layer_seq(fox_s, bd, t_new, FOX_HEADS)
    dk_s, dv_s = per_layer_seq(diff_s, bd, t_new, DIFF_HEADS)
    return (hf.reshape(b, s_len, d), hs.reshape(bd, t_new, d), fk_p, fv_p, jnp.stack(logf_p), dk_p, dv_p,
            fk_s, fv_s, jnp.stack(logf_s), dk_s, dv_s)
```
